```python
import math
import jax
import jax.numpy as jnp
from jax import lax
import numpy as np

D_MODEL = 4096
BATCH = 2
SEQ = 8192
DEPTH = 4

N_MIXERS = 3
N_MLA = (DEPTH + 2) // N_MIXERS
N_S5 = (DEPTH + 1) // N_MIXERS
N_RET = DEPTH // N_MIXERS

MLA_HEADS = 64
MLA_Q_RANK = 1024
MLA_KV_RANK = 512
MLA_NOPE = 128
MLA_ROPE = 64
MLA_V = 128
Q_BLOCK = 128

S5_GROUP = 16
S5_GROUPS = D_MODEL // S5_GROUP
S5_STATE = 64
S5_DT_MIN = 1e-3
S5_DT_MAX = 1e-1

RET_HEADS = 16
RET_QK = D_MODEL // RET_HEADS
RET_V = 2 * D_MODEL // RET_HEADS
RET_CHUNK = 128

MEM_LEN = 256
XA_HEADS = 4
XA_DIM = 128

D_FF = 4 * D_MODEL
ROPE_BASE = 10000.0
NORM_EPS = 1e-6
POS_OFFSET_MAX = 1024

kernel_name = 'hybrid_mla_s5_retention_trunk'


def rms_norm(x, gain):
    xf = x.astype(jnp.float32)
    y = xf * lax.rsqrt(jnp.mean(xf * xf, axis=-1, keepdims=True) + NORM_EPS)
    return (y * gain.astype(jnp.float32)).astype(x.dtype)


def rope(x, pos):
    d = x.shape[-1]
    inv_freq = 1.0 / (ROPE_BASE ** (jnp.arange(0, d, 2, dtype=jnp.float32) / d))
    ang = pos.astype(jnp.float32)[:, :, None, None] * inv_freq
    cos, sin = jnp.cos(ang), jnp.sin(ang)
    xf = x.astype(jnp.float32)
    x1, x2 = xf[..., : d // 2], xf[..., d // 2:]
    return jnp.concatenate([x1 * cos - x2 * sin, x2 * cos + x1 * sin], axis=-1).astype(x.dtype)


def causal_block_attention(q_nope, q_pe, k_nope, k_pe, v, scale):
    b, s, h, _ = q_nope.shape
    nb = s // Q_BLOCK

    def to_blocks(t):
        return t.reshape(b, nb, Q_BLOCK, *t.shape[2:]).swapaxes(0, 1)

    key_idx = jnp.arange(s)

    def one_block(args):
        blk, qn, qp = args
        scores = (jnp.einsum('bqhd,bkhd->bhqk', qn, k_nope, preferred_element_type=jnp.float32)
                  + jnp.einsum('bqhd,bkd->bhqk', qp, k_pe, preferred_element_type=jnp.float32)) * scale
        q_idx = blk * Q_BLOCK + jnp.arange(Q_BLOCK)
        causal = key_idx[None, :] <= q_idx[:, None]
        scores = jnp.where(causal, scores, -jnp.inf)
        p = jax.nn.softmax(scores, axis=-1).astype(v.dtype)
        return jnp.einsum('bhqk,bkhd->bqhd', p, v)

    out = lax.map(one_block, (jnp.arange(nb), to_blocks(q_nope), to_blocks(q_pe)))
    return out.swapaxes(0, 1).reshape(b, s, h, v.shape[-1])


def mla_mixer(h, pos, wq_a, q_norm, wq_b, wkv_a, kv_norm, wkv_b, wo):
    b, s, _ = h.shape
    c_q = rms_norm(h @ wq_a, q_norm)
    q = (c_q @ wq_b).reshape(b, s, MLA_HEADS, MLA_NOPE + MLA_ROPE)
    q_nope, q_pe = q[..., :MLA_NOPE], rope(q[..., MLA_NOPE:], pos)
    kv_a = h @ wkv_a
    c_kv = rms_norm(kv_a[..., :MLA_KV_RANK], kv_norm)
    k_pe = rope(kv_a[..., MLA_KV_RANK:][:, :, None, :], pos)[:, :, 0]
    kv = (c_kv @ wkv_b).reshape(b, s, MLA_HEADS, MLA_NOPE + MLA_V)
    k_nope, v = kv[..., :MLA_NOPE], kv[..., MLA_NOPE:]
    o = causal_block_attention(q_nope, q_pe, k_nope, k_pe, v, (MLA_NOPE + MLA_ROPE) ** -0.5)
    return o.reshape(b, s, MLA_HEADS * MLA_V) @ wo


def s5_mixer(h, lam_re, lam_im, log_dt, b_re, b_im, c_re, c_im, d_skip, w_glu):
    b, s, _ = h.shape
    f32 = jnp.float32
    u = h.astype(f32).reshape(b, s, S5_GROUPS, S5_GROUP)
    lam = lax.complex(lam_re.astype(f32), lam_im.astype(f32))
    dt = jnp.exp(log_dt.astype(f32))[:, None]
    lam_bar = jnp.exp(lam * dt)
    b_mat = lax.complex(b_re.astype(f32), b_im.astype(f32))
    b_bar = ((lam_bar - 1.0) / lam)[..., None] * b_mat
    bu = jnp.einsum('gph,bsgh->bsgp', b_bar, u.astype(jnp.complex64))

    def combine(left, right):
        a_l, x_l = left
        a_r, x_r = right
        return a_r * a_l, a_r * x_l + x_r

    def scan_one(bu_seq):
        a = jnp.broadcast_to(lam_bar, bu_seq.shape)
        _, xs = lax.associative_scan(combine, (a, bu_seq), axis=0)
        return xs

    states = jax.vmap(scan_one)(bu)
    c_mat = lax.complex(c_re.astype(f32), c_im.astype(f32))
    y = jnp.einsum('ghp,bsgp->bsgh', c_mat, states).real.reshape(b, s, D_MODEL)
    y = (y + d_skip.astype(f32) * h.astype(f32)).astype(h.dtype)
    val, gate = jnp.split(y @ w_glu, 2, axis=-1)
    return val * jax.nn.sigmoid(gate)


def retention_mixer(h, pos, wq, wk, wv, wg, gn, wo):
    b, s, _ = h.shape
    f32 = jnp.float32
    nc = s // RET_CHUNK
    q = rope((h @ wq).reshape(b, s, RET_HEADS, RET_QK), pos).astype(f32)
    k = rope((h @ wk).reshape(b, s, RET_HEADS, RET_QK), pos).astype(f32) * RET_QK ** -0.5
    v = (h @ wv).reshape(b, s, RET_HEADS, RET_V).astype(f32)

    def to_chunks(t):
        return t.reshape(b, nc, RET_CHUNK, RET_HEADS, t.shape[-1]).transpose(1, 0, 3, 2, 4)

    qc, kc, vc = to_chunks(q), to_chunks(k), to_chunks(v)
    log_gamma = jnp.log1p(-jnp.exp2(-5.0 - jnp.arange(RET_HEADS, dtype=f32)))
    idx = jnp.arange(RET_CHUNK, dtype=f32)
    rel = idx[:, None] - idx[None, :]
    decay = jnp.where(rel >= 0, jnp.exp(log_gamma[:, None, None] * jnp.maximum(rel, 0.0)), 0.0)
    inner = jnp.einsum('nbhqd,nbhkd->nbhqk', qc, kc) * decay
    inner_out = jnp.einsum('nbhqk,nbhkv->nbhqv', inner, vc)
    q_decay = jnp.exp(log_gamma[:, None] * (idx + 1.0))[:, :, None]
    k_decay = jnp.exp(log_gamma[:, None] * (RET_CHUNK - 1.0 - idx))[:, :, None]
    chunk_decay = jnp.exp(log_gamma * RET_CHUNK)[:, None, None]

    def step(state, chunk):
        q_n, k_n, v_n = chunk
        cross = jnp.einsum('bhqd,bhdv->bhqv', q_n, state) * q_decay
        state = state * chunk_decay + jnp.einsum('bhkd,bhkv->bhdv', k_n * k_decay, v_n)
        return state, cross

    state0 = jnp.zeros((b, RET_HEADS, RET_QK, RET_V), f32)
    _, cross_out = lax.scan(step, state0, (qc, kc, vc))
    y = (inner_out + cross_out).transpose(1, 0, 3, 2, 4).reshape(b, s, RET_HEADS, RET_V)
    mu = jnp.mean(y, axis=-1, keepdims=True)
    var = jnp.mean(jnp.square(y - mu), axis=-1, keepdims=True)
    y = ((y - mu) * lax.rsqrt(var + NORM_EPS)).reshape(b, s, RET_HEADS * RET_V) * gn.astype(f32)
    return (jax.nn.silu(h @ wg) * y.astype(h.dtype)) @ wo


def memory_cross_attention(h, mem_n, wq, wk, wv, wo):
    b, s, _ = h.shape
    m = mem_n.shape[1]
    q = (h @ wq).reshape(b, s, XA_HEADS, XA_DIM)
    k = (mem_n @ wk).reshape(b, m, XA_HEADS, XA_DIM)
    v = (mem_n @ wv).reshape(b, m, XA_HEADS, XA_DIM)
    scores = jnp.einsum('bshd,bmhd->bhsm', q, k, preferred_element_type=jnp.float32) * XA_DIM ** -0.5
    p = jax.nn.softmax(scores, axis=-1).astype(v.dtype)
    o = jnp.einsum('bhsm,bmhd->bshd', p, v).reshape(b, s, XA_HEADS * XA_DIM)
    return o @ wo


def squared_relu_mlp(h, w1, w2):
    return jnp.square(jax.nn.relu(h @ w1)) @ w2


def setup_inputs(seed: int = 0) -> dict:
    key = jax.random.key(seed)
    keys = iter(jax.random.split(key, 40))
    f32 = jnp.float32

    def normal(shape, scale):
        return jax.random.normal(next(keys), shape, f32) * scale

    def gain(shape):
        return 1.0 + normal(shape, 0.02)

    d, ff = D_MODEL, D_FF
    xa = XA_HEADS * XA_DIM
    mla_qk = MLA_NOPE + MLA_ROPE
    x = normal((BATCH, SEQ, d), 1.0)
    mem = normal((BATCH, MEM_LEN, d), 1.0)
    offsets = jax.random.randint(next(keys), (BATCH, 1), 0, POS_OFFSET_MAX, dtype=jnp.int32)
    positions = offsets + jnp.arange(SEQ, dtype=jnp.int32)[None, :]
    lam_im0 = math.pi * jnp.arange(S5_STATE, dtype=f32)
    s5_state_shape = (N_S5, S5_GROUPS, S5_STATE)
    return {
        'x': x,
        'mem': mem,
        'positions': positions,
        'norm_gain': gain((DEPTH, 6, d)),
        'mem_norm': gain((d,)),
        'xa_wq': normal((DEPTH, d, xa), d ** -0.5),
        'xa_wk': normal((DEPTH, d, xa), d ** -0.5),
        'xa_wv': normal((DEPTH, d, xa), d ** -0.5),
        'xa_wo': normal((DEPTH, xa, d), xa ** -0.5),
        'mlp_w1': normal((DEPTH, d, ff), d ** -0.5),
        'mlp_w2': normal((DEPTH, ff, d), ff ** -0.5),
        'mla_wq_a': normal((N_MLA, d, MLA_Q_RANK), d ** -0.5),
        'mla_q_norm': gain((N_MLA, MLA_Q_RANK)),
        'mla_wq_b': normal((N_MLA, MLA_Q_RANK, MLA_HEADS * mla_qk), MLA_Q_RANK ** -0.5),
        'mla_wkv_a': normal((N_MLA, d, MLA_KV_RANK + MLA_ROPE), d ** -0.5),
        'mla_kv_norm': gain((N_MLA, MLA_KV_RANK)),
        'mla_wkv_b': normal((N_MLA, MLA_KV_RANK, MLA_HEADS * (MLA_NOPE + MLA_V)), MLA_KV_RANK ** -0.5),
        'mla_wo': normal((N_MLA, MLA_HEADS * MLA_V, d), (MLA_HEADS * MLA_V) ** -0.5),
        's5_lam_re': -0.5 + normal(s5_state_shape, 0.01),
        's5_lam_im': lam_im0 + normal(s5_state_shape, 0.01),
        's5_log_dt': jax.random.uniform(next(keys), (N_S5, S5_GROUPS), f32,
                                        math.log(S5_DT_MIN), math.log(S5_DT_MAX)),
        's5_b_re': normal((N_S5, S5_GROUPS, S5_STATE, S5_GROUP), (2 * S5_GROUP) ** -0.5),
        's5_b_im': normal((N_S5, S5_GROUPS, S5_STATE, S5_GROUP), (2 * S5_GROUP) ** -0.5),
        's5_c_re': normal((N_S5, S5_GROUPS, S5_GROUP, S5_STATE), S5_STATE ** -0.5),
        's5_c_im': normal((N_S5, S5_GROUPS, S5_GROUP, S5_STATE), S5_STATE ** -0.5),
        's5_d': normal((N_S5, d), 1.0),
        's5_w_glu': normal((N_S5, d, 2 * d), d ** -0.5),
        'ret_wq': normal((N_RET, d, d), d ** -0.5),
        'ret_wk': normal((N_RET, d, d), d ** -0.5),
        'ret_wv': normal((N_RET, d, 2 * d), d ** -0.5),
        'ret_wg': normal((N_RET, d, 2 * d), d ** -0.5),
        'ret_gn': gain((N_RET, 2 * d)),
        'ret_wo': normal((N_RET, 2 * d, d), (2 * d) ** -0.5),
    }


def reference(x, mem, positions, norm_gain, mem_norm, xa_wq, xa_wk, xa_wv, xa_wo, mlp_w1, mlp_w2,
              mla_wq_a, mla_q_norm, mla_wq_b, mla_wkv_a, mla_kv_norm, mla_wkv_b, mla_wo,
              s5_lam_re, s5_lam_im, s5_log_dt, s5_b_re, s5_b_im, s5_c_re, s5_c_im, s5_d, s5_w_glu,
              ret_wq, ret_wk, ret_wv, ret_wg, ret_gn, ret_wo):
    mem_n = rms_norm(mem, mem_norm)
    h = x
    for i in range(DEPTH):
        kind, slot = i % N_MIXERS, i // N_MIXERS
        g = norm_gain[i]
        t = rms_norm(h, g[0])
        if kind == 0:
            t = mla_mixer(t, positions, mla_wq_a[slot], mla_q_norm[slot], mla_wq_b[slot],
                          mla_wkv_a[slot], mla_kv_norm[slot], mla_wkv_b[slot], mla_wo[slot])
        elif kind == 1:
            t = s5_mixer(t, s5_lam_re[slot], s5_lam_im[slot], s5_log_dt[slot], s5_b_re[slot],
                         s5_b_im[slot], s5_c_re[slot], s5_c_im[slot], s5_d[slot], s5_w_glu[slot])
        else:
            t = retention_mixer(t, positions, ret_wq[slot], ret_wk[slot], ret_wv[slot],
                                ret_wg[slot], ret_gn[slot], ret_wo[slot])
        h = h + rms_norm(t, g[1])
        t = memory_cross_attention(rms_norm(h, g[2]), mem_n, xa_wq[i], xa_wk[i], xa_wv[i], xa_wo[i])
        h = h + rms_norm(t, g[3])
        t = squared_relu_mlp(rms_norm(h, g[4]), mlp_w1[i], mlp_w2[i])
        h = h + rms_norm(t, g[5])
    return h
```

```python
import functools
import math

import jax
import jax.numpy as jnp
from jax import lax
from jax.experimental import pallas as pl
from jax.experimental.pallas import tpu as pltpu

F32 = jnp.float32
BF16 = jnp.bfloat16

V7X_VMEM_LIMIT_BYTES = 56 * 1024 * 1024
LANES = 128

MLA_HEADS = 64
MLA_NOPE = 128
MLA_ROPE = 64
MLA_V = 128
S5_GROUP = 16
S5_STATE = 64
S5_GROUPS_PER_BLOCK = 16
RET_HEADS = 16
RET_CHUNK = 128
XA_HEADS = 4
XA_DIM = 128
ROPE_BASE = 10000.0
NORM_EPS = 1e-6


def _params(*sem):
    return pltpu.CompilerParams(dimension_semantics=sem,
                                vmem_limit_bytes=V7X_VMEM_LIMIT_BYTES)


def _rms(x, g):
    return x * lax.rsqrt(jnp.mean(x * x, axis=-1, keepdims=True) + NORM_EPS) * g


def _rmsnorm_kernel(x_ref, g_ref, o_ref):
    o_ref[...] = _rms(x_ref[...].astype(F32), g_ref[...]).astype(o_ref.dtype)


def rmsnorm(x, g, out_dtype=BF16, tm=512):
    m, d = x.shape
    tm = min(tm, m)
    return pl.pallas_call(
        _rmsnorm_kernel,
        grid=(m // tm,),
        in_specs=[pl.BlockSpec((tm, d), lambda i: (i, 0)),
                  pl.BlockSpec((1, d), lambda i: (0, 0))],
        out_specs=pl.BlockSpec((tm, d), lambda i: (i, 0)),
        out_shape=jax.ShapeDtypeStruct((m, d), out_dtype),
        compiler_params=_params("parallel"),
        name="rmsnorm",
    )(x, g.reshape(1, d).astype(F32))


def _residual_kernel(h_ref, t_ref, gpost_ref, gpre_ref, hout_ref, xn_ref):
    h = h_ref[...] + _rms(t_ref[...].astype(F32), gpost_ref[...])
    hout_ref[...] = h
    xn_ref[...] = _rms(h, gpre_ref[...]).astype(xn_ref.dtype)


def _residual_last_kernel(h_ref, t_ref, gpost_ref, hout_ref):
    hout_ref[...] = h_ref[...] + _rms(t_ref[...].astype(F32), gpost_ref[...])


def residual_norm(h, t, g_post, g_pre=None, tm=256):
    m, d = h.shape
    tm = min(tm, m)
    row = pl.BlockSpec((tm, d), lambda i: (i, 0))
    vec = pl.BlockSpec((1, d), lambda i: (0, 0))
    if g_pre is None:
        return pl.pallas_call(
            _residual_last_kernel, grid=(m // tm,),
            in_specs=[row, row, vec], out_specs=row,
            out_shape=jax.ShapeDtypeStruct((m, d), F32),
            compiler_params=_params("parallel"), name="residual_last",
        )(h, t, g_post.reshape(1, d).astype(F32))
    return pl.pallas_call(
        _residual_kernel, grid=(m // tm,),
        in_specs=[row, row, vec, vec], out_specs=[row, row],
        out_shape=[jax.ShapeDtypeStruct((m, d), F32), jax.ShapeDtypeStruct((m, d), BF16)],
        compiler_params=_params("parallel"), name="residual_norm",
    )(h, t, g_post.reshape(1, d).astype(F32), g_pre.reshape(1, d).astype(F32))


def _activation(y, act):
    if act == "relu2":
        r = jnp.maximum(y, 0.0)
        return r * r
    if act == "silu":
        return y * jax.nn.sigmoid(y)
    assert act is None
    return y


def _mm_kernel(*refs, nk, act, scale, gated):
    if gated:
        x_ref, w_ref, wg_ref, o_ref = refs[:4]
        scratch = refs[4:]
    else:
        x_ref, w_ref, o_ref = refs[:3]
        wg_ref = None
        scratch = refs[3:]

    def finish(y, yg):
        if scale != 1.0:
            y = y * scale
        if gated:
            y = y * jax.nn.sigmoid(yg)
        o_ref[...] = _activation(y, act).astype(o_ref.dtype)

    x = x_ref[...]
    y = jnp.dot(x, w_ref[...], preferred_element_type=F32)
    yg = jnp.dot(x, wg_ref[...], preferred_element_type=F32) if gated else None
    if nk == 1:
        finish(y, yg)
        return

    k = pl.program_id(2)
    acc_ref = scratch[0]
    accg_ref = scratch[1] if gated else None

    @pl.when(k == 0)
    def _():
        acc_ref[...] = y
        if gated:
            accg_ref[...] = yg

    @pl.when(jnp.logical_and(k > 0, k < nk - 1))
    def _():
        acc_ref[...] += y
        if gated:
            accg_ref[...] += yg

    @pl.when(k == nk - 1)
    def _():
        finish(acc_ref[...] + y, accg_ref[...] + yg if gated else None)


def matmul(x, w, *, out_dtype, act=None, scale=1.0, w_gate=None, tm=1024, tn=1024, tk=4096):
    m, kdim = x.shape
    n = w.shape[1]
    tm, tn, tk = min(tm, m), min(tn, n), min(tk, kdim)
    assert m % tm == 0 and n % tn == 0 and kdim % tk == 0
    nk = kdim // tk
    gated = w_gate is not None
    x_spec = pl.BlockSpec((tm, tk), lambda i, j, k: (i, k))
    w_spec = pl.BlockSpec((tk, tn), lambda i, j, k: (k, j))
    operands = [x, w] + ([w_gate] if gated else [])
    in_specs = [x_spec, w_spec] + ([w_spec] if gated else [])
    scratch = []
    if nk > 1:
        scratch = [pltpu.VMEM((tm, tn), F32)] * (2 if gated else 1)
    return pl.pallas_call(
        functools.partial(_mm_kernel, nk=nk, act=act, scale=scale, gated=gated),
        grid=(m // tm, n // tn, nk),
        in_specs=in_specs,
        out_specs=pl.BlockSpec((tm, tn), lambda i, j, k: (i, j)),
        out_shape=jax.ShapeDtypeStruct((m, n), out_dtype),
        scratch_shapes=scratch,
        compiler_params=_params("parallel", "parallel", "arbitrary"),
        name="matmul",
    )(*operands)


def _rope_table_kernel(pos_ref, freq_ref, sign_ref, cos_ref, sin_ref):
    ang = pos_ref[...].astype(F32) * freq_ref[...]
    cos_ref[...] = jnp.cos(ang)
    sin_ref[...] = jnp.sin(ang) * sign_ref[...]


def rope_tables(pos, inv_freq_lanes, sign_lanes, tm=1024):
    n = pos.shape[0]
    tm = min(tm, n)
    out = jax.ShapeDtypeStruct((n, LANES), F32)
    return pl.pallas_call(
        _rope_table_kernel, grid=(n // tm,),
        in_specs=[pl.BlockSpec((tm, 1), lambda i: (i, 0)),
                  pl.BlockSpec((1, LANES), lambda i: (0, 0)),
                  pl.BlockSpec((1, LANES), lambda i: (0, 0))],
        out_specs=[pl.BlockSpec((tm, LANES), lambda i: (i, 0))] * 2,
        out_shape=[out, out],
        compiler_params=_params("parallel"), name="rope_tables",
    )(pos.reshape(n, 1), inv_freq_lanes.reshape(1, LANES), sign_lanes.reshape(1, LANES))


def _swap_half_pairs(x):
    lane = lax.broadcasted_iota(jnp.int32, x.shape, 1)
    return jnp.where(lane % MLA_ROPE < MLA_ROPE // 2,
                     pltpu.roll(x, LANES - MLA_ROPE // 2, 1),
                     pltpu.roll(x, MLA_ROPE // 2, 1))


def _rope_pairs_kernel(x_ref, cos_ref, sin_ref, o_ref):
    x = x_ref[...].astype(F32)
    o_ref[...] = (x * cos_ref[...] + _swap_half_pairs(x) * sin_ref[...]).astype(o_ref.dtype)


def rope_pairs(x, col_block0, n_col_blocks, cos, sin, tm=1024):
    n = x.shape[0]
    tm = min(tm, n)
    return pl.pallas_call(
        _rope_pairs_kernel, grid=(n // tm, n_col_blocks),
        in_specs=[pl.BlockSpec((tm, LANES), lambda i, j: (i, col_block0 + j)),
                  pl.BlockSpec((tm, LANES), lambda i, j: (i, 0)),
                  pl.BlockSpec((tm, LANES), lambda i, j: (i, 0))],
        out_specs=pl.BlockSpec((tm, LANES), lambda i, j: (i, j)),
        out_shape=jax.ShapeDtypeStruct((n, n_col_blocks * LANES), BF16),
        compiler_params=_params("parallel", "arbitrary"), name="rope_pairs",
    )(x, cos, sin)


def _mla_attn_kernel(qn_ref, qp_ref, kn_ref, v_ref, kp_ref, o_ref,
                     kcat_ref, m_ref, l_ref, acc_ref, *, tq):
    qi = pl.program_id(2)
    s_len = kn_ref.shape[0]
    lane_kp = lax.broadcasted_iota(jnp.int32, (s_len, LANES), 1)

    @pl.when(qi == 0)
    def _():
        kp = kp_ref[...]
        for h in range(2):
            kcat_ref[h, :, :MLA_NOPE] = kn_ref[:, h * MLA_NOPE:(h + 1) * MLA_NOPE]
            keep = (lane_kp < MLA_ROPE) if h == 0 else (lane_kp >= MLA_ROPE)
            kcat_ref[h, :, MLA_NOPE:] = jnp.where(keep, kp, jnp.zeros_like(kp))

    lane_q = lax.broadcasted_iota(jnp.int32, (tq, LANES), 1)
    row = lax.broadcasted_iota(jnp.int32, (tq, tq), 0)
    col = lax.broadcasted_iota(jnp.int32, (tq, tq), 1)
    qp = qp_ref[...]

    for h in range(2):
        keep = (lane_q < MLA_ROPE) if h == 0 else (lane_q >= MLA_ROPE)
        qcat = jnp.concatenate(
            [qn_ref[:, h * MLA_NOPE:(h + 1) * MLA_NOPE], jnp.where(keep, qp, jnp.zeros_like(qp))],
            axis=1)
        m_ref[...] = jnp.full(m_ref.shape, -jnp.inf, F32)
        l_ref[...] = jnp.zeros(l_ref.shape, F32)
        acc_ref[...] = jnp.zeros(acc_ref.shape, F32)

        def block(j, masked):
            start = pl.multiple_of(j * tq, tq)
            k = kcat_ref[h, pl.ds(start, tq), :]
            v = v_ref[pl.ds(start, tq), h * MLA_V:(h + 1) * MLA_V]
            s = lax.dot_general(qcat, k, (((1,), (1,)), ((), ())), preferred_element_type=F32)
            if masked:
                s = jnp.where(col <= row, s, -jnp.inf)
            m_prev = m_ref[...]
            m_new = jnp.maximum(m_prev, jnp.max(s, axis=1, keepdims=True))
            alpha = jnp.exp(m_prev - m_new)
            p = jnp.exp(s - m_new[:, :1])
            l_ref[...] = alpha * l_ref[...] + jnp.sum(p, axis=1, keepdims=True)
            acc_ref[...] = alpha * acc_ref[...] + jnp.dot(p.astype(BF16), v,
                                                          preferred_element_type=F32)
            m_ref[...] = m_new

        def body(j, carry):
            block(j, False)
            return carry

        lax.fori_loop(0, qi, body, 0)
        block(qi, True)
        o_ref[:, h * MLA_V:(h + 1) * MLA_V] = (acc_ref[...] / l_ref[...]).astype(o_ref.dtype)


def mla_attention(q_all, q_pe, kv_all, k_pe, batch, seq, tq=512):
    n = q_all.shape[0]
    tq = min(tq, seq)
    nq = seq // tq
    pair = 2 * MLA_NOPE
    n_pairs = MLA_HEADS // 2
    return pl.pallas_call(
        functools.partial(_mla_attn_kernel, tq=tq),
        grid=(batch, n_pairs, nq),
        in_specs=[
            pl.BlockSpec((tq, pair), lambda b, hp, i: (b * nq + i, hp)),
            pl.BlockSpec((tq, LANES), lambda b, hp, i: (b * nq + i, hp)),
            pl.BlockSpec((seq, pair), lambda b, hp, i: (b, hp)),
            pl.BlockSpec((seq, pair), lambda b, hp, i: (b, n_pairs + hp)),
            pl.BlockSpec((seq, LANES), lambda b, hp, i: (b, 0)),
        ],
        out_specs=pl.BlockSpec((tq, pair), lambda b, hp, i: (b * nq + i, hp)),
        out_shape=jax.ShapeDtypeStruct((n, MLA_HEADS * MLA_V), BF16),
        scratch_shapes=[pltpu.VMEM((2, seq, 2 * LANES), BF16),
                        pltpu.VMEM((tq, LANES), F32),
                        pltpu.VMEM((tq, LANES), F32),
                        pltpu.VMEM((tq, MLA_V), F32)],
        compiler_params=_params("parallel", "parallel", "arbitrary"),
        name="mla_attention",
    )(q_all, q_pe, kv_all, kv_all, k_pe)


def _xattn_kernel(x_ref, wq_ref, k_ref, v_ref, wo_ref, o_ref):
    q = jnp.dot(x_ref[...], wq_ref[...], preferred_element_type=F32).astype(BF16)
    heads = []
    for h in range(XA_HEADS):
        sl = slice(h * XA_DIM, (h + 1) * XA_DIM)
        s = lax.dot_general(q[:, sl], k_ref[:, sl], (((1,), (1,)), ((), ())),
                            preferred_element_type=F32) * (XA_DIM ** -0.5)
        p = jnp.exp(s - jnp.max(s, axis=1, keepdims=True))
        p = (p / jnp.sum(p, axis=1, keepdims=True)).astype(BF16)
        heads.append(jnp.dot(p, v_ref[:, sl], preferred_element_type=F32).astype(BF16))
    o = jnp.concatenate(heads, axis=1)
    o_ref[...] = jnp.dot(o, wo_ref[...], preferred_element_type=F32)


def cross_attention(xn, wq, k, v, wo, seq, mem_len, tm=512):
    n, d = xn.shape
    xa = wq.shape[1]
    tm = min(tm, seq)
    steps_per_batch = seq // tm
    return pl.pallas_call(
        _xattn_kernel, grid=(n // tm,),
        in_specs=[pl.BlockSpec((tm, d), lambda i: (i, 0)),
                  pl.BlockSpec((d, xa), lambda i: (0, 0)),
                  pl.BlockSpec((mem_len, xa), lambda i: (i // steps_per_batch, 0)),
                  pl.BlockSpec((mem_len, xa), lambda i: (i // steps_per_batch, 0)),
                  pl.BlockSpec((xa, d), lambda i: (0, 0))],
        out_specs=pl.BlockSpec((tm, d), lambda i: (i, 0)),
        out_shape=jax.ShapeDtypeStruct((n, d), F32),
        compiler_params=_params("parallel"), name="cross_attention",
    )(xn, wq, k, v, wo)


S5_SUB = 8


def _cmul(ar, ai, br, bi):
    return ar * br - ai * bi, ar * bi + ai * br


def _s5_kernel(u_ref, lre_ref, lim_ref, ldt_ref, bre_ref, bim_ref, cre_ref, cim_ref, d_ref,
               y_ref, wbr_ref, wbi_ref, xr_ref, xi_ref, zr_ref, zi_ref, pr_ref, pi_ref,
               car_ref, cai_ref, *, tt):
    t = pl.program_id(2)
    nlb = xr_ref.shape[0]
    nchunk = tt // S5_SUB
    lanes = [slice(lb * LANES, (lb + 1) * LANES) for lb in range(nlb)]

    lre, lim = lre_ref[...], lim_ref[...]
    dt = jnp.exp(ldt_ref[...])
    mag = jnp.exp(lre * dt)
    a_re, a_im = mag * jnp.cos(lim * dt), mag * jnp.sin(lim * dt)

    @pl.when(t == 0)
    def _():
        den = lre * lre + lim * lim
        nr, ni = a_re - 1.0, a_im
        cr = (nr * lre + ni * lim) / den
        ci = (ni * lre - nr * lim) / den
        for lb in range(nlb):
            bre, bim = bre_ref[:, lanes[lb]], bim_ref[:, lanes[lb]]
            wbr_ref[:, lanes[lb]] = (cr[lb] * bre - ci[lb] * bim).astype(BF16)
            wbi_ref[:, lanes[lb]] = (cr[lb] * bim + ci[lb] * bre).astype(BF16)
        car_ref[...] = jnp.zeros(car_ref.shape, F32)
        cai_ref[...] = jnp.zeros(cai_ref.shape, F32)

    u = u_ref[...]
    bu_re = jnp.dot(u, wbr_ref[...], preferred_element_type=F32)
    bu_im = jnp.dot(u, wbi_ref[...], preferred_element_type=F32)
    for lb in range(nlb):
        xr_ref[lb] = bu_re[:, lanes[lb]]
        xi_ref[lb] = bu_im[:, lanes[lb]]

    pw = [(a_re, a_im)]
    for _ in range(S5_SUB - 1):
        pw.append(_cmul(pw[-1][0], pw[-1][1], a_re, a_im))
    a8_re, a8_im = pw[-1]

    for lb in range(nlb):
        ar, ai = a_re[lb], a_im[lb]
        pr = xr_ref[lb, pl.ds(0, nchunk, stride=S5_SUB), :]
        pi = xi_ref[lb, pl.ds(0, nchunk, stride=S5_SUB), :]
        for r in range(1, S5_SUB):
            rows = pl.ds(r, nchunk, stride=S5_SUB)
            mr, mi = _cmul(ar, ai, pr, pi)
            pr = mr + xr_ref[lb, rows, :]
            pi = mi + xi_ref[lb, rows, :]
            xr_ref[lb, rows, :] = pr
            xi_ref[lb, rows, :] = pi
        zr_ref[lb] = pr
        zi_ref[lb] = pi

    def chunk_step(c, carry):
        cr, ci = carry
        pr_ref[:, pl.ds(c, 1), :] = cr
        pi_ref[:, pl.ds(c, 1), :] = ci
        mr, mi = _cmul(a8_re, a8_im, cr, ci)
        return mr + zr_ref[:, pl.ds(c, 1), :], mi + zi_ref[:, pl.ds(c, 1), :]

    cr, ci = lax.fori_loop(0, nchunk, chunk_step, (car_ref[...], cai_ref[...]))
    car_ref[...] = cr
    cai_ref[...] = ci

    for lb in range(nlb):
        er, ei = pr_ref[lb], pi_ref[lb]
        for r in range(S5_SUB):
            rows = pl.ds(r, nchunk, stride=S5_SUB)
            mr, mi = _cmul(pw[r][0][lb], pw[r][1][lb], er, ei)
            xr_ref[lb, rows, :] = xr_ref[lb, rows, :] + mr
            xi_ref[lb, rows, :] = xi_ref[lb, rows, :] + mi

    xs_re = jnp.concatenate([xr_ref[lb] for lb in range(nlb)], axis=1).astype(BF16)
    xs_im = jnp.concatenate([xi_ref[lb] for lb in range(nlb)], axis=1).astype(BF16)
    y = (jnp.dot(xs_re, cre_ref[...], preferred_element_type=F32)
         - jnp.dot(xs_im, cim_ref[...], preferred_element_type=F32))
    y_ref[...] = (y + d_ref[...] * u.astype(F32)).astype(y_ref.dtype)


def s5_scan(u, lam_re, lam_im, log_dt, b_re, b_im, c_re, c_im, d_skip, batch, seq, tt=512):
    n, d = u.shape
    groups, state = lam_re.shape
    gb = S5_GROUPS_PER_BLOCK
    nblk = groups // gb
    width = gb * state
    cols = gb * S5_GROUP
    tt = min(tt, seq)
    nt = seq // tt
    eye = jnp.eye(gb, dtype=F32)

    def block_diag_b(b):
        b4 = b.astype(F32).reshape(nblk, gb, state, S5_GROUP).transpose(0, 1, 3, 2)
        return (b4[:, :, :, None, :] * eye[None, :, None, :, None]).reshape(nblk * cols, width)

    def block_diag_c(c):
        c4 = c.astype(F32).reshape(nblk, gb, S5_GROUP, state).transpose(0, 1, 3, 2)
        return (c4[:, :, :, None, :] * eye[None, :, None, :, None]).reshape(nblk * width, cols).astype(BF16)

    nlb = width // LANES
    flat = lambda a: a.astype(F32).reshape(nblk * nlb, 1, LANES)
    ldt = flat(jnp.broadcast_to(log_dt.astype(F32)[:, None], (groups, state)))
    vec = pl.BlockSpec((nlb, 1, LANES), lambda j, b, t: (j, 0, 0))
    return pl.pallas_call(
        functools.partial(_s5_kernel, tt=tt),
        grid=(nblk, batch, nt),
        in_specs=[pl.BlockSpec((tt, cols), lambda j, b, t: (b * nt + t, j)),
                  vec, vec, vec,
                  pl.BlockSpec((cols, width), lambda j, b, t: (j, 0)),
                  pl.BlockSpec((cols, width), lambda j, b, t: (j, 0)),
                  pl.BlockSpec((width, cols), lambda j, b, t: (j, 0)),
                  pl.BlockSpec((width, cols), lambda j, b, t: (j, 0)),
                  pl.BlockSpec((1, cols), lambda j, b, t: (0, j))],
        out_specs=pl.BlockSpec((tt, cols), lambda j, b, t: (b * nt + t, j)),
        out_shape=jax.ShapeDtypeStruct((n, d), BF16),
        scratch_shapes=[pltpu.VMEM((cols, width), BF16), pltpu.VMEM((cols, width), BF16),
                        pltpu.VMEM((nlb, tt, LANES), F32), pltpu.VMEM((nlb, tt, LANES), F32),
                        pltpu.VMEM((nlb, tt // S5_SUB, LANES), F32), pltpu.VMEM((nlb, tt // S5_SUB, LANES), F32),
                        pltpu.VMEM((nlb, tt // S5_SUB, LANES), F32), pltpu.VMEM((nlb, tt // S5_SUB, LANES), F32),
                        pltpu.VMEM((nlb, 1, LANES), F32), pltpu.VMEM((nlb, 1, LANES), F32)],
        compiler_params=_params("parallel", "arbitrary", "arbitrary"),
        name="s5_scan",
    )(u, flat(lam_re), flat(lam_im), ldt, block_diag_b(b_re), block_diag_b(b_im),
      block_diag_c(c_re), block_diag_c(c_im), d_skip.astype(F32).reshape(1, d))


def _retention_kernel(q_ref, k_ref, v_ref, g_ref, cos_ref, sin_ref, gn_ref, lg_ref, o_ref,
                      state_ref, *, chunks):
    c = RET_CHUNK
    half = q_ref.shape[1] // 2

    @pl.when(pl.program_id(2) == 0)
    def _():
        state_ref[...] = jnp.zeros(state_ref.shape, F32)

    lg = lg_ref[0][:1, :1]
    ri = lax.broadcasted_iota(jnp.int32, (c, c), 0)
    ci = lax.broadcasted_iota(jnp.int32, (c, c), 1)
    rel = (ri - ci).astype(F32)
    decay = jnp.where(rel >= 0, jnp.exp(lg * jnp.maximum(rel, 0.0)), 0.0)
    idx = lax.broadcasted_iota(jnp.int32, (c, 1), 0).astype(F32)
    q_decay = jnp.exp(lg * (idx + 1.0))
    k_decay = jnp.exp(lg * (c - 1.0 - idx))
    chunk_decay = jnp.exp(lg * c)
    k_scale = q_ref.shape[1] ** -0.5

    def rope(x, cos, sin):
        x1, x2 = x[:, :half], x[:, half:]
        return jnp.concatenate([x1 * cos - x2 * sin, x2 * cos + x1 * sin], axis=1)

    for ch in range(chunks):
        rows = slice(ch * c, (ch + 1) * c)
        cos, sin = cos_ref[rows, :], sin_ref[rows, :]
        q = rope(q_ref[rows, :].astype(F32), cos, sin)
        k = rope(k_ref[rows, :].astype(F32), cos, sin) * k_scale
        v = v_ref[rows, :]
        qb = q.astype(BF16)
        inner = lax.dot_general(qb, k.astype(BF16), (((1,), (1,)), ((), ())),
                                preferred_element_type=F32) * decay
        y = jnp.dot(inner.astype(BF16), v, preferred_element_type=F32)
        state = state_ref[...]
        y = y + jnp.dot(qb, state.astype(BF16), preferred_element_type=F32) * q_decay
        kd = (k * k_decay).astype(BF16)
        state_ref[...] = state * chunk_decay + lax.dot_general(
            kd, v, (((0,), (0,)), ((), ())), preferred_element_type=F32)
        mu = jnp.mean(y, axis=1, keepdims=True)
        yc = y - mu
        var = jnp.mean(yc * yc, axis=1, keepdims=True)
        yn = yc * lax.rsqrt(var + NORM_EPS) * gn_ref[...]
        o_ref[rows, :] = (g_ref[rows, :].astype(F32) * yn).astype(o_ref.dtype)


def retention(qk, v, g, cos, sin, gn, log_gamma_tiles, batch, seq, rows_per_step=512):
    n = qk.shape[0]
    dqk = qk.shape[1] // (2 * RET_HEADS)
    dv = v.shape[1] // RET_HEADS
    rows = min(rows_per_step, seq)
    ns = seq // rows
    row_qk = lambda off: pl.BlockSpec((rows, dqk), lambda b, h, i: (b * ns + i, off + h))
    row_v = pl.BlockSpec((rows, dv), lambda b, h, i: (b * ns + i, h))
    row_t = pl.BlockSpec((rows, LANES), lambda b, h, i: (b * ns + i, 0))
    return pl.pallas_call(
        functools.partial(_retention_kernel, chunks=rows // RET_CHUNK),
        grid=(batch, RET_HEADS, ns),
        in_specs=[row_qk(0), row_qk(RET_HEADS), row_v, row_v, row_t, row_t,
                  pl.BlockSpec((1, dv), lambda b, h, i: (0, h)),
                  pl.BlockSpec((1, 8, LANES), lambda b, h, i: (h, 0, 0))],
        out_specs=row_v,
        out_shape=jax.ShapeDtypeStruct((n, RET_HEADS * dv), BF16),
        scratch_shapes=[pltpu.VMEM((dqk, dv), F32)],
        compiler_params=_params("parallel", "parallel", "arbitrary"),
        name="retention",
    )(qk, qk, v, g, cos, sin, gn.astype(F32).reshape(1, -1), log_gamma_tiles)


def _mla_mixer(tn, rope_cs, wq_a, q_norm, wq_b, wkv_a, kv_norm, wkv_b, wo, batch, seq):
    kv_rank = kv_norm.shape[0]
    qk_dim = MLA_NOPE + MLA_ROPE
    wq_b3 = wq_b.reshape(-1, MLA_HEADS, qk_dim)
    wq_b_perm = jnp.concatenate([wq_b3[:, :, :MLA_NOPE].reshape(-1, MLA_HEADS * MLA_NOPE),
                                 wq_b3[:, :, MLA_NOPE:].reshape(-1, MLA_HEADS * MLA_ROPE)], axis=1)
    wkv_b3 = wkv_b.reshape(-1, MLA_HEADS, MLA_NOPE + MLA_V)
    wkv_b_perm = jnp.concatenate([wkv_b3[:, :, :MLA_NOPE].reshape(-1, MLA_HEADS * MLA_NOPE),
                                  wkv_b3[:, :, MLA_NOPE:].reshape(-1, MLA_HEADS * MLA_V)], axis=1)
    w_kpe = wkv_a[:, kv_rank:]
    w_kpe2 = jnp.concatenate([w_kpe, w_kpe], axis=1)

    c_q = matmul(tn, wq_a.astype(BF16), out_dtype=F32)
    c_q = rmsnorm(c_q, q_norm)
    q_all = matmul(c_q, wq_b_perm.astype(BF16), out_dtype=BF16, scale=qk_dim ** -0.5)
    c_kv = matmul(tn, wkv_a[:, :kv_rank].astype(BF16), out_dtype=F32)
    c_kv = rmsnorm(c_kv, kv_norm)
    kv_all = matmul(c_kv, wkv_b_perm.astype(BF16), out_dtype=BF16)
    k_pe = matmul(tn, w_kpe2.astype(BF16), out_dtype=F32)

    cos, sin = rope_cs
    nope_blocks = MLA_HEADS * MLA_NOPE // LANES
    q_pe = rope_pairs(q_all, nope_blocks, MLA_HEADS * MLA_ROPE // LANES, cos, sin)
    k_pe = rope_pairs(k_pe, 0, 1, cos, sin)
    o = mla_attention(q_all, q_pe, kv_all, k_pe, batch, seq)
    return matmul(o, wo.astype(BF16), out_dtype=F32)


def _s5_mixer(tn, lam_re, lam_im, log_dt, b_re, b_im, c_re, c_im, d_skip, w_glu, batch, seq):
    d = tn.shape[1]
    y = s5_scan(tn, lam_re, lam_im, log_dt, b_re, b_im, c_re, c_im, d_skip, batch, seq)
    return matmul(y, w_glu[:, :d].astype(BF16), w_gate=w_glu[:, d:].astype(BF16),
                  out_dtype=F32, tn=512)


def _retention_mixer(tn, rope_cs, wq, wk, wv, wg, gn, wo, log_gamma_tiles, batch, seq):
    cos, sin = rope_cs
    qk = matmul(tn, jnp.concatenate([wq, wk], axis=1).astype(BF16), out_dtype=BF16)
    v = matmul(tn, wv.astype(BF16), out_dtype=BF16)
    g = matmul(tn, wg.astype(BF16), out_dtype=BF16, act="silu")
    y = retention(qk, v, g, cos, sin, gn, log_gamma_tiles, batch, seq)
    return matmul(y, wo.astype(BF16), out_dtype=F32)


def _rope_freqs(dim):
    return 1.0 / (ROPE_BASE ** (jnp.arange(0, dim, 2, dtype=F32) / dim))


def kernel(x, mem, positions, norm_gain, mem_norm, xa_wq, xa_wk, xa_wv, xa_wo, mlp_w1, mlp_w2,
           mla_wq_a, mla_q_norm, mla_wq_b, mla_wkv_a, mla_kv_norm, mla_wkv_b, mla_wo,
           s5_lam_re, s5_lam_im, s5_log_dt, s5_b_re, s5_b_im, s5_c_re, s5_c_im, s5_d, s5_w_glu,
           ret_wq, ret_wk, ret_wv, ret_wg, ret_gn, ret_wo):
    batch, seq, d = x.shape
    n = batch * seq
    depth = norm_gain.shape[0]
    mem_len = mem.shape[1]
    n_mixers = 3

    pos = positions.reshape(n)
    f_mla = jnp.tile(_rope_freqs(MLA_ROPE), 4)
    sign_mla = jnp.tile(jnp.concatenate([-jnp.ones(MLA_ROPE // 2, F32), jnp.ones(MLA_ROPE // 2, F32)]), 2)
    mla_cs = rope_tables(pos, f_mla, sign_mla)
    ret_cs = rope_tables(pos, _rope_freqs(d // RET_HEADS), jnp.ones(LANES, F32))
    log_gamma = jnp.log1p(-jnp.exp2(-5.0 - jnp.arange(RET_HEADS, dtype=F32)))
    log_gamma_tiles = jnp.broadcast_to(log_gamma[:, None, None], (RET_HEADS, 8, LANES))

    mem_n = rmsnorm(mem.reshape(batch * mem_len, d), mem_norm)

    h = x.reshape(n, d)
    tn = rmsnorm(h, norm_gain[0, 0])
    for i in range(depth):
        kind, slot = i % n_mixers, i // n_mixers
        g = norm_gain[i]
        if kind == 0:
            t = _mla_mixer(tn, mla_cs, mla_wq_a[slot], mla_q_norm[slot], mla_wq_b[slot],
                           mla_wkv_a[slot], mla_kv_norm[slot], mla_wkv_b[slot], mla_wo[slot],
                           batch, seq)
        elif kind == 1:
            t = _s5_mixer(tn, s5_lam_re[slot], s5_lam_im[slot], s5_log_dt[slot], s5_b_re[slot],
                          s5_b_im[slot], s5_c_re[slot], s5_c_im[slot], s5_d[slot], s5_w_glu[slot],
                          batch, seq)
        else:
            t = _retention_mixer(tn, ret_cs, ret_wq[slot], ret_wk[slot], ret_wv[slot], ret_wg[slot],
                                 ret_gn[slot], ret_wo[slot], log_gamma_tiles, batch, seq)
        h, tn = residual_norm(h, t, g[1], g[2])

        k_mem = matmul(mem_n, xa_wk[i].astype(BF16), out_dtype=BF16)
        v_mem = matmul(mem_n, xa_wv[i].astype(BF16), out_dtype=BF16)
        t = cross_attention(tn, xa_wq[i].astype(BF16), k_mem, v_mem, xa_wo[i].astype(BF16),
                            seq, mem_len)
        h, tn = residual_norm(h, t, g[3], g[4])

        hid = matmul(tn, mlp_w1[i].astype(BF16), out_dtype=BF16, act="relu2")
        t = matmul(hid, mlp_w2[i].astype(BF16), out_dtype=F32)
        if i + 1 < depth:
            h, tn = residual_norm(h, t, g[5], norm_gain[i + 1, 0])
        else:
            h = residual_norm(h, t, g[5])
    return h.reshape(batch, seq, d)
```

```python
import functools
import math

import jax
import jax.numpy as jnp
from jax import lax
from jax.experimental import pallas as pl
from jax.experimental.pallas import tpu as pltpu

F32 = jnp.float32
BF16 = jnp.bfloat16

V7X_VMEM_LIMIT_BYTES = 56 * 1024 * 1024
LANES = 128

MLA_HEADS = 64
MLA_NOPE = 128
MLA_ROPE = 64
MLA_V = 128
S5_GROUP = 16
S5_STATE = 64
S5_GROUPS_PER_BLOCK = 16
RET_HEADS = 16
RET_CHUNK = 128
XA_HEADS = 4
XA_DIM = 128
ROPE_BASE = 10000.0
NORM_EPS = 1e-6


def _params(*sem):
    return pltpu.CompilerParams(dimension_semantics=sem,
                                vmem_limit_bytes=V7X_VMEM_LIMIT_BYTES)


def _rms(x, g):
    return x * lax.rsqrt(jnp.mean(x * x, axis=-1, keepdims=True) + NORM_EPS) * g


def _rmsnorm_kernel(x_ref, g_ref, o_ref):
    o_ref[...] = _rms(x_ref[...].astype(F32), g_ref[...]).astype(o_ref.dtype)


def rmsnorm(x, g, out_dtype=BF16, tm=512):
    m, d = x.shape
    tm = min(tm, m)
    return pl.pallas_call(
        _rmsnorm_kernel,
        grid=(m // tm,),
        in_specs=[pl.BlockSpec((tm, d), lambda i: (i, 0)),
                  pl.BlockSpec((1, d), lambda i: (0, 0))],
        out_specs=pl.BlockSpec((tm, d), lambda i: (i, 0)),
        out_shape=jax.ShapeDtypeStruct((m, d), out_dtype),
        compiler_params=_params("parallel"),
        name="rmsnorm",
    )(x, g.reshape(1, d).astype(F32))


def _residual_kernel(h_ref, t_ref, gpost_ref, gpre_ref, hout_ref, xn_ref):
    h = h_ref[...] + _rms(t_ref[...].astype(F32), gpost_ref[...])
    hout_ref[...] = h
    xn_ref[...] = _rms(h, gpre_ref[...]).astype(xn_ref.dtype)


def _residual_last_kernel(h_ref, t_ref, gpost_ref, hout_ref):
    hout_ref[...] = h_ref[...] + _rms(t_ref[...].astype(F32), gpost_ref[...])


def residual_norm(h, t, g_post, g_pre=None, tm=256):
    m, d = h.shape
    tm = min(tm, m)
    row = pl.BlockSpec((tm, d), lambda i: (i, 0))
    vec = pl.BlockSpec((1, d), lambda i: (0, 0))
    if g_pre is None:
        return pl.pallas_call(
            _residual_last_kernel, grid=(m // tm,),
            in_specs=[row, row, vec], out_specs=row,
            out_shape=jax.ShapeDtypeStruct((m, d), F32),
            compiler_params=_params("parallel"), name="residual_last",
        )(h, t, g_post.reshape(1, d).astype(F32))
    return pl.pallas_call(
        _residual_kernel, grid=(m // tm,),
        in_specs=[row, row, vec, vec], out_specs=[row, row],
        out_shape=[jax.ShapeDtypeStruct((m, d), F32), jax.ShapeDtypeStruct((m, d), BF16)],
        compiler_params=_params("parallel"), name="residual_norm",
    )(h, t, g_post.reshape(1, d).astype(F32), g_pre.reshape(1, d).astype(F32))


def _activation(y, act):
    if act == "relu2":
        r = jnp.maximum(y, 0.0)
        return r * r
    if act == "silu":
        return y * jax.nn.sigmoid(y)
    assert act is None
    return y


def _mm_kernel(*refs, nk, act, scale, gated):
    if gated:
        x_ref, w_ref, wg_ref, o_ref = refs[:4]
        scratch = refs[4:]
    else:
        x_ref, w_ref, o_ref = refs[:3]
        wg_ref = None
        scratch = refs[3:]

    def finish(y, yg):
        if scale != 1.0:
            y = y * scale
        if gated:
            y = y * jax.nn.sigmoid(yg)
        o_ref[...] = _activation(y, act).astype(o_ref.dtype)

    x = x_ref[...]
    y = jnp.dot(x, w_ref[...], preferred_element_type=F32)
    yg = jnp.dot(x, wg_ref[...], preferred_element_type=F32) if gated else None
    if nk == 1:
        finish(y, yg)
        return

    k = pl.program_id(2)
    acc_ref = scratch[0]
    accg_ref = scratch[1] if gated else None

    @pl.when(k == 0)
    def _():
        acc_ref[...] = y
        if gated:
            accg_ref[...] = yg

    @pl.when(jnp.logical_and(k > 0, k < nk - 1))
    def _():
        acc_ref[...] += y
        if gated:
            accg_ref[...] += yg

    @pl.when(k == nk - 1)
    def _():
        finish(acc_ref[...] + y, accg_ref[...] + yg if gated else None)


def matmul(x, w, *, out_dtype, act=None, scale=1.0, w_gate=None, tm=1024, tn=1024, tk=4096):
    m, kdim = x.shape
    n = w.shape[1]
    tm, tn, tk = min(tm, m), min(tn, n), min(tk, kdim)
    assert m % tm == 0 and n % tn == 0 and kdim % tk == 0
    nk = kdim // tk
    gated = w_gate is not None
    x_spec = pl.BlockSpec((tm, tk), lambda i, j, k: (i, k))
    w_spec = pl.BlockSpec((tk, tn), lambda i, j, k: (k, j))
    operands = [x, w] + ([w_gate] if gated else [])
    in_specs = [x_spec, w_spec] + ([w_spec] if gated else [])
    scratch = []
    if nk > 1:
        scratch = [pltpu.VMEM((tm, tn), F32)] * (2 if gated else 1)
    return pl.pallas_call(
        functools.partial(_mm_kernel, nk=nk, act=act, scale=scale, gated=gated),
        grid=(m // tm, n // tn, nk),
        in_specs=in_specs,
        out_specs=pl.BlockSpec((tm, tn), lambda i, j, k: (i, j)),
        out_shape=jax.ShapeDtypeStruct((m, n), out_dtype),
        scratch_shapes=scratch,
        compiler_params=_params("parallel", "parallel", "arbitrary"),
        name="matmul",
    )(*operands)


def _rope_table_kernel(pos_ref, freq_ref, sign_ref, cos_ref, sin_ref):
    ang = pos_ref[...].astype(F32) * freq_ref[...]
    cos_ref[...] = jnp.cos(ang)
    sin_ref[...] = jnp.sin(ang) * sign_ref[...]


def rope_tables(pos, inv_freq_lanes, sign_lanes, tm=1024):
    n = pos.shape[0]
    tm = min(tm, n)
    out = jax.ShapeDtypeStruct((n, LANES), F32)
    return pl.pallas_call(
        _rope_table_kernel, grid=(n // tm,),
        in_specs=[pl.BlockSpec((tm, 1), lambda i: (i, 0)),
                  pl.BlockSpec((1, LANES), lambda i: (0, 0)),
                  pl.BlockSpec((1, LANES), lambda i: (0, 0))],
        out_specs=[pl.BlockSpec((tm, LANES), lambda i: (i, 0))] * 2,
        out_shape=[out, out],
        compiler_params=_params("parallel"), name="rope_tables",
    )(pos.reshape(n, 1), inv_freq_lanes.reshape(1, LANES), sign_lanes.reshape(1, LANES))


def _swap_half_pairs(x):
    lane = lax.broadcasted_iota(jnp.int32, x.shape, 1)
    return jnp.where(lane % MLA_ROPE < MLA_ROPE // 2,
                     pltpu.roll(x, LANES - MLA_ROPE // 2, 1),
                     pltpu.roll(x, MLA_ROPE // 2, 1))


def _rope_pairs_kernel(x_ref, cos_ref, sin_ref, o_ref):
    x = x_ref[...].astype(F32)
    o_ref[...] = (x * cos_ref[...] + _swap_half_pairs(x) * sin_ref[...]).astype(o_ref.dtype)


def rope_pairs(x, col_block0, n_col_blocks, cos, sin, tm=1024):
    n = x.shape[0]
    tm = min(tm, n)
    return pl.pallas_call(
        _rope_pairs_kernel, grid=(n // tm, n_col_blocks),
        in_specs=[pl.BlockSpec((tm, LANES), lambda i, j: (i, col_block0 + j)),
                  pl.BlockSpec((tm, LANES), lambda i, j: (i, 0)),
                  pl.BlockSpec((tm, LANES), lambda i, j: (i, 0))],
        out_specs=pl.BlockSpec((tm, LANES), lambda i, j: (i, j)),
        out_shape=jax.ShapeDtypeStruct((n, n_col_blocks * LANES), BF16),
        compiler_params=_params("parallel", "arbitrary"), name="rope_pairs",
    )(x, cos, sin)


ATTN_KEYS_PER_TILE = 256
ATTN_QUERIES_PER_TILE = 256
ATTN_SCORE_LOOKAHEAD = 8


def _mla_attn_kernel(qn_ref, qp_ref, kn_ref, v_ref, kp_ref, o_ref,
                     kcat_ref, vt_ref, qt_ref, acc_ref, *, tq):
    ksub, qstrip = min(ATTN_KEYS_PER_TILE, tq), min(ATTN_QUERIES_PER_TILE, tq)
    qi = pl.program_id(2)
    nblk = kn_ref.shape[0] // tq
    nsub = tq // ksub
    chains = [(h, t) for h in range(2) for t in range(tq // qstrip)]

    @pl.when(qi == 0)
    def _():
        lane = lax.broadcasted_iota(jnp.int32, (tq, LANES), 1)
        for c in range(nblk):
            rows = slice(c * tq, (c + 1) * tq)
            kp = kp_ref[rows, :]
            for h in range(2):
                kcat_ref[h, rows, :MLA_NOPE] = kn_ref[rows, h * MLA_NOPE:(h + 1) * MLA_NOPE]
                keep = (lane < MLA_ROPE) if h == 0 else (lane >= MLA_ROPE)
                kcat_ref[h, rows, MLA_NOPE:] = jnp.where(keep, kp, jnp.zeros_like(kp))
                vt_ref[h, c] = v_ref[rows, h * MLA_V:(h + 1) * MLA_V].T

    lane_q = lax.broadcasted_iota(jnp.int32, (tq, LANES), 1)
    qp = qp_ref[...]
    for h in range(2):
        keep = (lane_q < MLA_ROPE) if h == 0 else (lane_q >= MLA_ROPE)
        qcat = jnp.concatenate(
            [qn_ref[:, h * MLA_NOPE:(h + 1) * MLA_NOPE], jnp.where(keep, qp, jnp.zeros_like(qp))],
            axis=1)
        qt_ref[h] = qcat.T
    acc_ref[...] = jnp.zeros(acc_ref.shape, F32)

    krow = lax.broadcasted_iota(jnp.int32, (ksub, qstrip), 0)
    qcol = lax.broadcasted_iota(jnp.int32, (ksub, qstrip), 1)

    def scores(ci, k, mask_offset):
        h, t = chains[ci]
        s = jnp.dot(k, qt_ref[h, :, t * qstrip:(t + 1) * qstrip], preferred_element_type=F32)
        if mask_offset is not None:
            s = jnp.where(krow + mask_offset <= qcol, s, -jnp.inf)
        return s

    def update(ci, carry, s, vt):
        m_prev, l_prev = carry
        m_new = jnp.maximum(m_prev, jnp.max(s, axis=0, keepdims=True))
        alpha = jnp.exp2(m_prev - m_new)
        p = jnp.exp2(s - m_new)
        l_new = alpha * l_prev + jnp.sum(p, axis=0, keepdims=True)
        acc_ref[ci] = alpha * acc_ref[ci] + jnp.dot(vt, p.astype(BF16), preferred_element_type=F32)
        return m_new, l_new

    def run_tiles(jblk, start, carries, tiles):
        carries = list(carries)
        pending = []
        for idx in range(len(tiles) + ATTN_SCORE_LOOKAHEAD):
            if idx < len(tiles):
                sub, ci, off = tiles[idx]
                k = kcat_ref[chains[ci][0], pl.ds(start + sub * ksub, ksub), :]
                pending.append(scores(ci, k, off))
            if idx >= ATTN_SCORE_LOOKAHEAD:
                done = idx - ATTN_SCORE_LOOKAHEAD
                sub, ci, off = tiles[done]
                vt = vt_ref[chains[ci][0], jblk, :, sub * ksub:(sub + 1) * ksub]
                carries[ci] = update(ci, carries[ci], pending[done], vt)
                pending[done] = None
        return tuple(carries)

    full_tiles = [(sub, ci, None) for sub in range(nsub) for ci in range(len(chains))]

    def full_block(j, carries):
        return run_tiles(j, pl.multiple_of(j * tq, tq), carries, full_tiles)

    init = tuple((jnp.full((1, qstrip), -jnp.inf, F32), jnp.zeros((1, qstrip), F32)) for _ in chains)
    carries = lax.fori_loop(0, qi, full_block, init)

    diag_tiles = []
    for sub in range(nsub):
        for ci, (h, t) in enumerate(chains):
            if sub * ksub > (t + 1) * qstrip - 1:
                continue
            off = sub * ksub - t * qstrip
            diag_tiles.append((sub, ci, off if off + ksub - 1 > 0 else None))
    carries = run_tiles(qi, pl.multiple_of(qi * tq, tq), carries, diag_tiles)

    for ci, (h, t) in enumerate(chains):
        o = (acc_ref[ci] / carries[ci][1]).T
        o_ref[t * qstrip:(t + 1) * qstrip, h * MLA_V:(h + 1) * MLA_V] = o.astype(o_ref.dtype)


def mla_attention(q_all, q_pe, kv_all, k_pe, batch, seq, tq=1024):
    n = q_all.shape[0]
    tq = min(tq, seq)
    qstrip = min(ATTN_QUERIES_PER_TILE, tq)
    nq = seq // tq
    pair = 2 * MLA_NOPE
    n_pairs = MLA_HEADS // 2
    return pl.pallas_call(
        functools.partial(_mla_attn_kernel, tq=tq),
        grid=(batch, n_pairs, nq),
        in_specs=[
            pl.BlockSpec((tq, pair), lambda b, hp, i: (b * nq + i, hp)),
            pl.BlockSpec((tq, LANES), lambda b, hp, i: (b * nq + i, hp)),
            pl.BlockSpec((seq, pair), lambda b, hp, i: (b, hp)),
            pl.BlockSpec((seq, pair), lambda b, hp, i: (b, n_pairs + hp)),
            pl.BlockSpec((seq, LANES), lambda b, hp, i: (b, 0)),
        ],
        out_specs=pl.BlockSpec((tq, pair), lambda b, hp, i: (b * nq + i, hp)),
        out_shape=jax.ShapeDtypeStruct((n, MLA_HEADS * MLA_V), BF16),
        scratch_shapes=[pltpu.VMEM((2, seq, 2 * LANES), BF16),
                        pltpu.VMEM((2, seq // tq, MLA_V, tq), BF16),
                        pltpu.VMEM((2, 2 * LANES, tq), BF16),
                        pltpu.VMEM((2 * (tq // qstrip), MLA_V, qstrip), F32)],
        compiler_params=_params("parallel", "parallel", "arbitrary"),
        name="mla_attention",
    )(q_all, q_pe, kv_all, kv_all, k_pe)


def _xattn_kernel(h_ref, x_ref, wq_ref, k_ref, v_ref, wo_ref, gpost_ref, gpre_ref, hout_ref, xn_ref):
    q = jnp.dot(x_ref[...], wq_ref[...], preferred_element_type=F32).astype(BF16)
    heads = []
    for h in range(XA_HEADS):
        sl = slice(h * XA_DIM, (h + 1) * XA_DIM)
        s = lax.dot_general(q[:, sl], k_ref[:, sl], (((1,), (1,)), ((), ())),
                            preferred_element_type=F32) * (XA_DIM ** -0.5)
        p = jnp.exp(s - jnp.max(s, axis=1, keepdims=True))
        p = (p / jnp.sum(p, axis=1, keepdims=True)).astype(BF16)
        heads.append(jnp.dot(p, v_ref[:, sl], preferred_element_type=F32).astype(BF16))
    o = jnp.concatenate(heads, axis=1)
    t = jnp.dot(o, wo_ref[...], preferred_element_type=F32)
    h = h_ref[...] + _rms(t, gpost_ref[...])
    hout_ref[...] = h
    xn_ref[...] = _rms(h, gpre_ref[...]).astype(xn_ref.dtype)


def cross_attention(h, xn, wq, k, v, wo, g_post, g_pre, seq, mem_len, tm=256):
    n, d = xn.shape
    xa = wq.shape[1]
    tm = min(tm, seq)
    steps_per_batch = seq // tm
    row = pl.BlockSpec((tm, d), lambda i: (i, 0))
    vec = pl.BlockSpec((1, d), lambda i: (0, 0))
    mem_blk = pl.BlockSpec((mem_len, xa), lambda i: (i // steps_per_batch, 0))
    return pl.pallas_call(
        _xattn_kernel, grid=(n // tm,),
        in_specs=[row, row,
                  pl.BlockSpec((d, xa), lambda i: (0, 0)),
                  mem_blk, mem_blk,
                  pl.BlockSpec((xa, d), lambda i: (0, 0)),
                  vec, vec],
        out_specs=[row, row],
        out_shape=[jax.ShapeDtypeStruct((n, d), F32), jax.ShapeDtypeStruct((n, d), BF16)],
        compiler_params=_params("parallel"), name="cross_attention",
    )(h, xn, wq, k, v, wo, g_post.reshape(1, d).astype(F32), g_pre.reshape(1, d).astype(F32))


S5_SUB = 8


def _cmul(ar, ai, br, bi):
    return ar * br - ai * bi, ar * bi + ai * br


def _s5_kernel(u_ref, lre_ref, lim_ref, ldt_ref, bre_ref, bim_ref, cre_ref, cim_ref, d_ref,
               y_ref, wbr_ref, wbi_ref, xr_ref, xi_ref, zr_ref, zi_ref, pr_ref, pi_ref,
               car_ref, cai_ref, *, tt):
    t = pl.program_id(2)
    nlb = xr_ref.shape[0]
    nchunk = tt // S5_SUB
    lanes = [slice(lb * LANES, (lb + 1) * LANES) for lb in range(nlb)]

    lre, lim = lre_ref[...], lim_ref[...]
    dt = jnp.exp(ldt_ref[...])
    mag = jnp.exp(lre * dt)
    a_re, a_im = mag * jnp.cos(lim * dt), mag * jnp.sin(lim * dt)

    @pl.when(t == 0)
    def _():
        den = lre * lre + lim * lim
        nr, ni = a_re - 1.0, a_im
        cr = (nr * lre + ni * lim) / den
        ci = (ni * lre - nr * lim) / den
        for lb in range(nlb):
            bre, bim = bre_ref[:, lanes[lb]], bim_ref[:, lanes[lb]]
            wbr_ref[:, lanes[lb]] = (cr[lb] * bre - ci[lb] * bim).astype(BF16)
            wbi_ref[:, lanes[lb]] = (cr[lb] * bim + ci[lb] * bre).astype(BF16)
        car_ref[...] = jnp.zeros(car_ref.shape, F32)
        cai_ref[...] = jnp.zeros(cai_ref.shape, F32)

    u = u_ref[...]
    bu_re = jnp.dot(u, wbr_ref[...], preferred_element_type=F32)
    bu_im = jnp.dot(u, wbi_ref[...], preferred_element_type=F32)
    for lb in range(nlb):
        xr_ref[lb] = bu_re[:, lanes[lb]]
        xi_ref[lb] = bu_im[:, lanes[lb]]

    pw = [(a_re, a_im)]
    for _ in range(S5_SUB - 1):
        pw.append(_cmul(pw[-1][0], pw[-1][1], a_re, a_im))
    a8_re, a8_im = pw[-1]

    for lb in range(nlb):
        ar, ai = a_re[lb], a_im[lb]
        pr = xr_ref[lb, pl.ds(0, nchunk, stride=S5_SUB), :]
        pi = xi_ref[lb, pl.ds(0, nchunk, stride=S5_SUB), :]
        for r in range(1, S5_SUB):
            rows = pl.ds(r, nchunk, stride=S5_SUB)
            mr, mi = _cmul(ar, ai, pr, pi)
            pr = mr + xr_ref[lb, rows, :]
            pi = mi + xi_ref[lb, rows, :]
            xr_ref[lb, rows, :] = pr
            xi_ref[lb, rows, :] = pi
        zr_ref[lb] = pr
        zi_ref[lb] = pi

    def chunk_step(c, carry):
        cr, ci = carry
        pr_ref[:, pl.ds(c, 1), :] = cr
        pi_ref[:, pl.ds(c, 1), :] = ci
        mr, mi = _cmul(a8_re, a8_im, cr, ci)
        return mr + zr_ref[:, pl.ds(c, 1), :], mi + zi_ref[:, pl.ds(c, 1), :]

    cr, ci = lax.fori_loop(0, nchunk, chunk_step, (car_ref[...], cai_ref[...]))
    car_ref[...] = cr
    cai_ref[...] = ci

    for lb in range(nlb):
        er, ei = pr_ref[lb], pi_ref[lb]
        for r in range(S5_SUB):
            rows = pl.ds(r, nchunk, stride=S5_SUB)
            mr, mi = _cmul(pw[r][0][lb], pw[r][1][lb], er, ei)
            xr_ref[lb, rows, :] = xr_ref[lb, rows, :] + mr
            xi_ref[lb, rows, :] = xi_ref[lb, rows, :] + mi

    xs_re = jnp.concatenate([xr_ref[lb] for lb in range(nlb)], axis=1).astype(BF16)
    xs_im = jnp.concatenate([xi_ref[lb] for lb in range(nlb)], axis=1).astype(BF16)
    y = (jnp.dot(xs_re, cre_ref[...], preferred_element_type=F32)
         - jnp.dot(xs_im, cim_ref[...], preferred_element_type=F32))
    y_ref[...] = (y + d_ref[...] * u.astype(F32)).astype(y_ref.dtype)


def s5_scan(u, lam_re, lam_im, log_dt, b_re, b_im, c_re, c_im, d_skip, batch, seq, tt=512):
    n, d = u.shape
    groups, state = lam_re.shape
    gb = S5_GROUPS_PER_BLOCK
    nblk = groups // gb
    width = gb * state
    cols = gb * S5_GROUP
    tt = min(tt, seq)
    nt = seq // tt
    eye = jnp.eye(gb, dtype=F32)

    def block_diag_b(b):
        b4 = b.astype(F32).reshape(nblk, gb, state, S5_GROUP).transpose(0, 1, 3, 2)
        return (b4[:, :, :, None, :] * eye[None, :, None, :, None]).reshape(nblk * cols, width)

    def block_diag_c(c):
        c4 = c.astype(F32).reshape(nblk, gb, S5_GROUP, state).transpose(0, 1, 3, 2)
        return (c4[:, :, :, None, :] * eye[None, :, None, :, None]).reshape(nblk * width, cols).astype(BF16)

    nlb = width // LANES
    flat = lambda a: a.astype(F32).reshape(nblk * nlb, 1, LANES)
    ldt = flat(jnp.broadcast_to(log_dt.astype(F32)[:, None], (groups, state)))
    vec = pl.BlockSpec((nlb, 1, LANES), lambda j, b, t: (j, 0, 0))
    return pl.pallas_call(
        functools.partial(_s5_kernel, tt=tt),
        grid=(nblk, batch, nt),
        in_specs=[pl.BlockSpec((tt, cols), lambda j, b, t: (b * nt + t, j)),
                  vec, vec, vec,
                  pl.BlockSpec((cols, width), lambda j, b, t: (j, 0)),
                  pl.BlockSpec((cols, width), lambda j, b, t: (j, 0)),
                  pl.BlockSpec((width, cols), lambda j, b, t: (j, 0)),
                  pl.BlockSpec((width, cols), lambda j, b, t: (j, 0)),
                  pl.BlockSpec((1, cols), lambda j, b, t: (0, j))],
        out_specs=pl.BlockSpec((tt, cols), lambda j, b, t: (b * nt + t, j)),
        out_shape=jax.ShapeDtypeStruct((n, d), BF16),
        scratch_shapes=[pltpu.VMEM((cols, width), BF16), pltpu.VMEM((cols, width), BF16),
                        pltpu.VMEM((nlb, tt, LANES), F32), pltpu.VMEM((nlb, tt, LANES), F32),
                        pltpu.VMEM((nlb, tt // S5_SUB, LANES), F32), pltpu.VMEM((nlb, tt // S5_SUB, LANES), F32),
                        pltpu.VMEM((nlb, tt // S5_SUB, LANES), F32), pltpu.VMEM((nlb, tt // S5_SUB, LANES), F32),
                        pltpu.VMEM((nlb, 1, LANES), F32), pltpu.VMEM((nlb, 1, LANES), F32)],
        compiler_params=_params("parallel", "arbitrary", "arbitrary"),
        name="s5_scan",
    )(u, flat(lam_re), flat(lam_im), ldt, block_diag_b(b_re), block_diag_b(b_im),
      block_diag_c(c_re), block_diag_c(c_im), d_skip.astype(F32).reshape(1, d))


def _retention_kernel(q_ref, k_ref, v_ref, g_ref, cos_ref, sin_ref, gn_ref, lg_ref, o_ref,
                      state_ref, *, chunks):
    c = RET_CHUNK
    half = q_ref.shape[1] // 2

    @pl.when(pl.program_id(2) == 0)
    def _():
        state_ref[...] = jnp.zeros(state_ref.shape, F32)

    lg = lg_ref[0][:1, :1]
    ri = lax.broadcasted_iota(jnp.int32, (c, c), 0)
    ci = lax.broadcasted_iota(jnp.int32, (c, c), 1)
    rel = (ri - ci).astype(F32)
    decay = jnp.where(rel >= 0, jnp.exp(lg * jnp.maximum(rel, 0.0)), 0.0)
    idx = lax.broadcasted_iota(jnp.int32, (c, 1), 0).astype(F32)
    q_decay = jnp.exp(lg * (idx + 1.0))
    k_decay = jnp.exp(lg * (c - 1.0 - idx))
    chunk_decay = jnp.exp(lg * c)
    k_scale = q_ref.shape[1] ** -0.5

    def rope(x, cos, sin):
        x1, x2 = x[:, :half], x[:, half:]
        return jnp.concatenate([x1 * cos - x2 * sin, x2 * cos + x1 * sin], axis=1)

    for ch in range(chunks):
        rows = slice(ch * c, (ch + 1) * c)
        cos, sin = cos_ref[rows, :], sin_ref[rows, :]
        q = rope(q_ref[rows, :].astype(F32), cos, sin)
        k = rope(k_ref[rows, :].astype(F32), cos, sin) * k_scale
        v = v_ref[rows, :]
        qb = q.astype(BF16)
        inner = lax.dot_general(qb, k.astype(BF16), (((1,), (1,)), ((), ())),
                                preferred_element_type=F32) * decay
        y = jnp.dot(inner.astype(BF16), v, preferred_element_type=F32)
        state = state_ref[...]
        y = y + jnp.dot(qb, state.astype(BF16), preferred_element_type=F32) * q_decay
        kd = (k * k_decay).astype(BF16)
        state_ref[...] = state * chunk_decay + lax.dot_general(
            kd, v, (((0,), (0,)), ((), ())), preferred_element_type=F32)
        mu = jnp.mean(y, axis=1, keepdims=True)
        yc = y - mu
        var = jnp.mean(yc * yc, axis=1, keepdims=True)
        yn = yc * lax.rsqrt(var + NORM_EPS) * gn_ref[...]
        o_ref[rows, :] = (g_ref[rows, :].astype(F32) * yn).astype(o_ref.dtype)


def retention(qk, v, g, cos, sin, gn, log_gamma_tiles, batch, seq, rows_per_step=512):
    n = qk.shape[0]
    dqk = qk.shape[1] // (2 * RET_HEADS)
    dv = v.shape[1] // RET_HEADS
    rows = min(rows_per_step, seq)
    ns = seq // rows
    row_qk = lambda off: pl.BlockSpec((rows, dqk), lambda b, h, i: (b * ns + i, off + h))
    row_v = pl.BlockSpec((rows, dv), lambda b, h, i: (b * ns + i, h))
    row_t = pl.BlockSpec((rows, LANES), lambda b, h, i: (b * ns + i, 0))
    return pl.pallas_call(
        functools.partial(_retention_kernel, chunks=rows // RET_CHUNK),
        grid=(batch, RET_HEADS, ns),
        in_specs=[row_qk(0), row_qk(RET_HEADS), row_v, row_v, row_t, row_t,
                  pl.BlockSpec((1, dv), lambda b, h, i: (0, h)),
                  pl.BlockSpec((1, 8, LANES), lambda b, h, i: (h, 0, 0))],
        out_specs=row_v,
        out_shape=jax.ShapeDtypeStruct((n, RET_HEADS * dv), BF16),
        scratch_shapes=[pltpu.VMEM((dqk, dv), F32)],
        compiler_params=_params("parallel", "parallel", "arbitrary"),
        name="retention",
    )(qk, qk, v, g, cos, sin, gn.astype(F32).reshape(1, -1), log_gamma_tiles)


def _mla_mixer(tn, rope_cs, wq_a, q_norm, wq_b, wkv_a, kv_norm, wkv_b, wo, batch, seq):
    kv_rank = kv_norm.shape[0]
    qk_dim = MLA_NOPE + MLA_ROPE
    wq_b3 = wq_b.reshape(-1, MLA_HEADS, qk_dim)
    wq_b_perm = jnp.concatenate([wq_b3[:, :, :MLA_NOPE].reshape(-1, MLA_HEADS * MLA_NOPE),
                                 wq_b3[:, :, MLA_NOPE:].reshape(-1, MLA_HEADS * MLA_ROPE)], axis=1)
    wkv_b3 = wkv_b.reshape(-1, MLA_HEADS, MLA_NOPE + MLA_V)
    wkv_b_perm = jnp.concatenate([wkv_b3[:, :, :MLA_NOPE].reshape(-1, MLA_HEADS * MLA_NOPE),
                                  wkv_b3[:, :, MLA_NOPE:].reshape(-1, MLA_HEADS * MLA_V)], axis=1)
    w_kpe = wkv_a[:, kv_rank:]
    w_kpe2 = jnp.concatenate([w_kpe, w_kpe], axis=1)

    c_q = matmul(tn, wq_a.astype(BF16), out_dtype=F32)
    c_q = rmsnorm(c_q, q_norm)
    q_all = matmul(c_q, wq_b_perm.astype(BF16), out_dtype=BF16, scale=qk_dim ** -0.5 * math.log2(math.e))
    c_kv = matmul(tn, wkv_a[:, :kv_rank].astype(BF16), out_dtype=F32)
    c_kv = rmsnorm(c_kv, kv_norm)
    kv_all = matmul(c_kv, wkv_b_perm.astype(BF16), out_dtype=BF16)
    k_pe = matmul(tn, w_kpe2.astype(BF16), out_dtype=F32)

    cos, sin = rope_cs
    nope_blocks = MLA_HEADS * MLA_NOPE // LANES
    q_pe = rope_pairs(q_all, nope_blocks, MLA_HEADS * MLA_ROPE // LANES, cos, sin)
    k_pe = rope_pairs(k_pe, 0, 1, cos, sin)
    o = mla_attention(q_all, q_pe, kv_all, k_pe, batch, seq)
    return matmul(o, wo.astype(BF16), out_dtype=F32)


def _s5_mixer(tn, lam_re, lam_im, log_dt, b_re, b_im, c_re, c_im, d_skip, w_glu, batch, seq):
    d = tn.shape[1]
    y = s5_scan(tn, lam_re, lam_im, log_dt, b_re, b_im, c_re, c_im, d_skip, batch, seq)
    return matmul(y, w_glu[:, :d].astype(BF16), w_gate=w_glu[:, d:].astype(BF16),
                  out_dtype=F32, tn=512)


def _retention_mixer(tn, rope_cs, wq, wk, wv, wg, gn, wo, log_gamma_tiles, batch, seq):
    cos, sin = rope_cs
    qk = matmul(tn, jnp.concatenate([wq, wk], axis=1).astype(BF16), out_dtype=BF16)
    v = matmul(tn, wv.astype(BF16), out_dtype=BF16)
    g = matmul(tn, wg.astype(BF16), out_dtype=BF16, act="silu")
    y = retention(qk, v, g, cos, sin, gn, log_gamma_tiles, batch, seq)
    return matmul(y, wo.astype(BF16), out_dtype=F32)


def _rope_freqs(dim):
    return 1.0 / (ROPE_BASE ** (jnp.arange(0, dim, 2, dtype=F32) / dim))


def kernel(x, mem, positions, norm_gain, mem_norm, xa_wq, xa_wk, xa_wv, xa_wo, mlp_w1, mlp_w2,
           mla_wq_a, mla_q_norm, mla_wq_b, mla_wkv_a, mla_kv_norm, mla_wkv_b, mla_wo,
           s5_lam_re, s5_lam_im, s5_log_dt, s5_b_re, s5_b_im, s5_c_re, s5_c_im, s5_d, s5_w_glu,
           ret_wq, ret_wk, ret_wv, ret_wg, ret_gn, ret_wo):
    batch, seq, d = x.shape
    n = batch * seq
    depth = norm_gain.shape[0]
    mem_len = mem.shape[1]
    n_mixers = 3

    pos = positions.reshape(n)
    f_mla = jnp.tile(_rope_freqs(MLA_ROPE), 4)
    sign_mla = jnp.tile(jnp.concatenate([-jnp.ones(MLA_ROPE // 2, F32), jnp.ones(MLA_ROPE // 2, F32)]), 2)
    mla_cs = rope_tables(pos, f_mla, sign_mla)
    ret_cs = rope_tables(pos, _rope_freqs(d // RET_HEADS), jnp.ones(LANES, F32))
    log_gamma = jnp.log1p(-jnp.exp2(-5.0 - jnp.arange(RET_HEADS, dtype=F32)))
    log_gamma_tiles = jnp.broadcast_to(log_gamma[:, None, None], (RET_HEADS, 8, LANES))

    mem_n = rmsnorm(mem.reshape(batch * mem_len, d), mem_norm)

    h = x.reshape(n, d)
    tn = rmsnorm(h, norm_gain[0, 0])
    for i in range(depth):
        kind, slot = i % n_mixers, i // n_mixers
        g = norm_gain[i]
        if kind == 0:
            t = _mla_mixer(tn, mla_cs, mla_wq_a[slot], mla_q_norm[slot], mla_wq_b[slot],
                           mla_wkv_a[slot], mla_kv_norm[slot], mla_wkv_b[slot], mla_wo[slot],
                           batch, seq)
        elif kind == 1:
            t = _s5_mixer(tn, s5_lam_re[slot], s5_lam_im[slot], s5_log_dt[slot], s5_b_re[slot],
                          s5_b_im[slot], s5_c_re[slot], s5_c_im[slot], s5_d[slot], s5_w_glu[slot],
                          batch, seq)
        else:
            t = _retention_mixer(tn, ret_cs, ret_wq[slot], ret_wk[slot], ret_wv[slot], ret_wg[slot],
                                 ret_gn[slot], ret_wo[slot], log_gamma_tiles, batch, seq)
        h, tn = residual_norm(h, t, g[1], g[2])

        k_mem = matmul(mem_n, xa_wk[i].astype(BF16), out_dtype=BF16)
        v_mem = matmul(mem_n, xa_wv[i].astype(BF16), out_dtype=BF16)
        h, tn = cross_attention(h, tn, xa_wq[i].astype(BF16), k_mem, v_mem, xa_wo[i].astype(BF16),
                                g[3], g[4], seq, mem_len)

        hid = matmul(tn, mlp_w1[i].astype(BF16), out_dtype=BF16, act="relu2")
        t = matmul(hid, mlp_w2[i].astype(BF16), out_dtype=F32)
        if i + 1 < depth:
            h, tn = residual_norm(h, t, g[5], norm_gain[i + 1, 0])
        else:
            h = residual_norm(h, t, g[5])
    return h.reshape(batch, seq, d)
```

```python
import functools
import math

import jax
import jax.numpy as jnp
from jax import lax
from jax.experimental import pallas as pl
from jax.experimental.pallas import tpu as pltpu

F32 = jnp.float32
BF16 = jnp.bfloat16

V7X_VMEM_LIMIT_BYTES = 56 * 1024 * 1024
LANES = 128

MLA_HEADS = 64
MLA_NOPE = 128
MLA_ROPE = 64
MLA_V = 128
S5_GROUP = 16
S5_STATE = 64
S5_GROUPS_PER_BLOCK = 16
RET_HEADS = 16
RET_CHUNK = 128
XA_HEADS = 4
XA_DIM = 128
ROPE_BASE = 10000.0
NORM_EPS = 1e-6


def _params(*sem):
    return pltpu.CompilerParams(dimension_semantics=sem,
                                vmem_limit_bytes=V7X_VMEM_LIMIT_BYTES)


def _rms(x, g):
    return x * lax.rsqrt(jnp.mean(x * x, axis=-1, keepdims=True) + NORM_EPS) * g


def _rmsnorm_kernel(x_ref, g_ref, o_ref):
    o_ref[...] = _rms(x_ref[...].astype(F32), g_ref[...]).astype(o_ref.dtype)


def rmsnorm(x, g, out_dtype=BF16, tm=512):
    m, d = x.shape
    tm = min(tm, m)
    return pl.pallas_call(
        _rmsnorm_kernel,
        grid=(m // tm,),
        in_specs=[pl.BlockSpec((tm, d), lambda i: (i, 0)),
                  pl.BlockSpec((1, d), lambda i: (0, 0))],
        out_specs=pl.BlockSpec((tm, d), lambda i: (i, 0)),
        out_shape=jax.ShapeDtypeStruct((m, d), out_dtype),
        compiler_params=_params("parallel"),
        name="rmsnorm",
    )(x, g.reshape(1, d).astype(F32))


def _residual_kernel(h_ref, t_ref, gpost_ref, gpre_ref, hout_ref, xn_ref):
    h = h_ref[...] + _rms(t_ref[...].astype(F32), gpost_ref[...])
    hout_ref[...] = h
    xn_ref[...] = _rms(h, gpre_ref[...]).astype(xn_ref.dtype)


def _residual_last_kernel(h_ref, t_ref, gpost_ref, hout_ref):
    hout_ref[...] = h_ref[...] + _rms(t_ref[...].astype(F32), gpost_ref[...])


def residual_norm(h, t, g_post, g_pre=None, tm=256):
    m, d = h.shape
    tm = min(tm, m)
    row = pl.BlockSpec((tm, d), lambda i: (i, 0))
    vec = pl.BlockSpec((1, d), lambda i: (0, 0))
    if g_pre is None:
        return pl.pallas_call(
            _residual_last_kernel, grid=(m // tm,),
            in_specs=[row, row, vec], out_specs=row,
            out_shape=jax.ShapeDtypeStruct((m, d), F32),
            compiler_params=_params("parallel"), name="residual_last",
        )(h, t, g_post.reshape(1, d).astype(F32))
    return pl.pallas_call(
        _residual_kernel, grid=(m // tm,),
        in_specs=[row, row, vec, vec], out_specs=[row, row],
        out_shape=[jax.ShapeDtypeStruct((m, d), F32), jax.ShapeDtypeStruct((m, d), BF16)],
        compiler_params=_params("parallel"), name="residual_norm",
    )(h, t, g_post.reshape(1, d).astype(F32), g_pre.reshape(1, d).astype(F32))


def _activation(y, act):
    if act == "relu2":
        r = jnp.maximum(y, 0.0)
        return r * r
    if act == "silu":
        return y * jax.nn.sigmoid(y)
    assert act is None
    return y


MXU_TILE_COLS = 256


def _mm_kernel(*refs, nk, act, scale, gated):
    if gated:
        x_ref, w_ref, wg_ref, o_ref = refs
    else:
        x_ref, w_ref, o_ref = refs
        wg_ref = None

    if nk == 1:
        x = x_ref[...]
        y = jnp.dot(x, w_ref[...], preferred_element_type=F32)
        if scale != 1.0:
            y = y * scale
        if gated:
            y = y * jax.nn.sigmoid(jnp.dot(x, wg_ref[...], preferred_element_type=F32))
        o_ref[...] = _activation(y, act).astype(o_ref.dtype)
        return

    k = pl.program_id(2)
    chunks = [slice(c, c + MXU_TILE_COLS) for c in range(0, o_ref.shape[1], MXU_TILE_COLS)]

    @pl.when(k == 0)
    def _():
        for cols in chunks:
            o_ref[:, cols] = jnp.dot(x_ref[...], w_ref[:, cols], preferred_element_type=F32)

    @pl.when(k > 0)
    def _():
        for cols in chunks:
            o_ref[:, cols] += jnp.dot(x_ref[...], w_ref[:, cols], preferred_element_type=F32)


def matmul(x, w, *, out_dtype, act=None, scale=1.0, layer=None, gate_col_block=None,
           tm=1024, tn=1024, tk=4096):
    m, kdim = x.shape
    gated = gate_col_block is not None
    n = w.shape[-1] // 2 if gated else w.shape[-1]
    tm, tn, tk = min(tm, m), min(tn, n), min(tk, kdim)
    assert m % tm == 0 and n % tn == 0 and kdim % tk == 0
    nk = kdim // tk
    if nk > 1:
        assert not gated and act is None and scale == 1.0 and out_dtype == F32
        assert tn % MXU_TILE_COLS == 0

    def w_spec(col0):
        if w.ndim == 3:
            return pl.BlockSpec((None, tk, tn), lambda i, j, k: (layer, k, col0 + j))
        return pl.BlockSpec((tk, tn), lambda i, j, k: (k, col0 + j))

    operands = [x, w] + ([w] if gated else [])
    in_specs = [pl.BlockSpec((tm, tk), lambda i, j, k: (i, k)), w_spec(0)]
    if gated:
        in_specs.append(w_spec(gate_col_block))
    return pl.pallas_call(
        functools.partial(_mm_kernel, nk=nk, act=act, scale=scale, gated=gated),
        grid=(m // tm, n // tn, nk),
        in_specs=in_specs,
        out_specs=pl.BlockSpec((tm, tn), lambda i, j, k: (i, j)),
        out_shape=jax.ShapeDtypeStruct((m, n), out_dtype),
        compiler_params=_params("parallel", "parallel", "arbitrary"),
        name="matmul",
    )(*operands)


def _rope_table_kernel(pos_ref, freq_ref, sign_ref, cos_ref, sin_ref):
    ang = pos_ref[...].astype(F32) * freq_ref[...]
    cos_ref[...] = jnp.cos(ang)
    sin_ref[...] = jnp.sin(ang) * sign_ref[...]


def rope_tables(pos, inv_freq_lanes, sign_lanes, tm=1024):
    n = pos.shape[0]
    tm = min(tm, n)
    out = jax.ShapeDtypeStruct((n, LANES), F32)
    return pl.pallas_call(
        _rope_table_kernel, grid=(n // tm,),
        in_specs=[pl.BlockSpec((tm, 1), lambda i: (i, 0)),
                  pl.BlockSpec((1, LANES), lambda i: (0, 0)),
                  pl.BlockSpec((1, LANES), lambda i: (0, 0))],
        out_specs=[pl.BlockSpec((tm, LANES), lambda i: (i, 0))] * 2,
        out_shape=[out, out],
        compiler_params=_params("parallel"), name="rope_tables",
    )(pos.reshape(n, 1), inv_freq_lanes.reshape(1, LANES), sign_lanes.reshape(1, LANES))


def _swap_half_pairs(x):
    lane = lax.broadcasted_iota(jnp.int32, x.shape, 1)
    return jnp.where(lane % MLA_ROPE < MLA_ROPE // 2,
                     pltpu.roll(x, LANES - MLA_ROPE // 2, 1),
                     pltpu.roll(x, MLA_ROPE // 2, 1))


def _rope_pairs_kernel(x_ref, cos_ref, sin_ref, o_ref):
    x = x_ref[...].astype(F32)
    o_ref[...] = (x * cos_ref[...] + _swap_half_pairs(x) * sin_ref[...]).astype(o_ref.dtype)


def rope_pairs(x, col_block0, n_col_blocks, cos, sin, tm=1024):
    n = x.shape[0]
    tm = min(tm, n)
    return pl.pallas_call(
        _rope_pairs_kernel, grid=(n // tm, n_col_blocks),
        in_specs=[pl.BlockSpec((tm, LANES), lambda i, j: (i, col_block0 + j)),
                  pl.BlockSpec((tm, LANES), lambda i, j: (i, 0)),
                  pl.BlockSpec((tm, LANES), lambda i, j: (i, 0))],
        out_specs=pl.BlockSpec((tm, LANES), lambda i, j: (i, j)),
        out_shape=jax.ShapeDtypeStruct((n, n_col_blocks * LANES), BF16),
        compiler_params=_params("parallel", "arbitrary"), name="rope_pairs",
    )(x, cos, sin)


ATTN_KEYS_PER_TILE = 256
ATTN_QUERIES_PER_TILE = 256
ATTN_SCORE_LOOKAHEAD = 8
ATTN_ONES_ROWS = 16


def _mla_attn_kernel(qn_ref, qp_ref, kn_ref, v_ref, kp_ref, o_ref,
                     kcat_ref, vt_ref, qt_ref, acc_ref, *, tq):
    ksub, qstrip = min(ATTN_KEYS_PER_TILE, tq), min(ATTN_QUERIES_PER_TILE, tq)
    qi = pl.program_id(2)
    nblk = kn_ref.shape[0] // tq
    nsub = tq // ksub
    chains = [(h, t) for h in range(2) for t in range(tq // qstrip)]

    @pl.when(qi == 0)
    def _():
        lane = lax.broadcasted_iota(jnp.int32, (tq, LANES), 1)
        for c in range(nblk):
            rows = slice(c * tq, (c + 1) * tq)
            kp = kp_ref[rows, :]
            for h in range(2):
                kcat_ref[h, rows, :MLA_NOPE] = kn_ref[rows, h * MLA_NOPE:(h + 1) * MLA_NOPE]
                keep = (lane < MLA_ROPE) if h == 0 else (lane >= MLA_ROPE)
                kcat_ref[h, rows, MLA_NOPE:] = jnp.where(keep, kp, jnp.zeros_like(kp))
                vt_ref[h, c, :MLA_V] = v_ref[rows, h * MLA_V:(h + 1) * MLA_V].T
                vt_ref[h, c, MLA_V:] = jnp.ones((ATTN_ONES_ROWS, tq), BF16)

    lane_q = lax.broadcasted_iota(jnp.int32, (tq, LANES), 1)
    qp = qp_ref[...]
    for h in range(2):
        keep = (lane_q < MLA_ROPE) if h == 0 else (lane_q >= MLA_ROPE)
        qcat = jnp.concatenate(
            [qn_ref[:, h * MLA_NOPE:(h + 1) * MLA_NOPE], jnp.where(keep, qp, jnp.zeros_like(qp))],
            axis=1)
        qt_ref[h] = qcat.T
    acc_ref[...] = jnp.zeros(acc_ref.shape, F32)

    krow = lax.broadcasted_iota(jnp.int32, (ksub, qstrip), 0)
    qcol = lax.broadcasted_iota(jnp.int32, (ksub, qstrip), 1)

    def scores(ci, k, mask_offset):
        h, t = chains[ci]
        s = jnp.dot(k, qt_ref[h, :, t * qstrip:(t + 1) * qstrip], preferred_element_type=F32)
        if mask_offset is not None:
            s = jnp.where(krow + mask_offset <= qcol, s, -jnp.inf)
        return s

    def update(ci, carry, s, vt):
        m_prev = carry
        m_new = jnp.maximum(m_prev, jnp.max(s, axis=0, keepdims=True))
        alpha = jnp.exp2(m_prev - m_new)
        p = jnp.exp2(s - m_new)
        acc_ref[ci] = alpha * acc_ref[ci] + jnp.dot(vt, p.astype(BF16), preferred_element_type=F32)
        return m_new

    def run_tiles(jblk, start, carries, tiles):
        carries = list(carries)
        pending = []
        for idx in range(len(tiles) + ATTN_SCORE_LOOKAHEAD):
            if idx < len(tiles):
                sub, ci, off = tiles[idx]
                k = kcat_ref[chains[ci][0], pl.ds(start + sub * ksub, ksub), :]
                pending.append(scores(ci, k, off))
            if idx >= ATTN_SCORE_LOOKAHEAD:
                done = idx - ATTN_SCORE_LOOKAHEAD
                sub, ci, off = tiles[done]
                vt = vt_ref[chains[ci][0], jblk, :, sub * ksub:(sub + 1) * ksub]
                carries[ci] = update(ci, carries[ci], pending[done], vt)
                pending[done] = None
        return tuple(carries)

    full_tiles = [(sub, ci, None) for sub in range(nsub) for ci in range(len(chains))]

    def full_block(j, carries):
        return run_tiles(j, pl.multiple_of(j * tq, tq), carries, full_tiles)

    init = tuple(jnp.full((1, qstrip), -jnp.inf, F32) for _ in chains)
    carries = lax.fori_loop(0, qi, full_block, init)

    diag_tiles = []
    for sub in range(nsub):
        for ci, (h, t) in enumerate(chains):
            if sub * ksub > (t + 1) * qstrip - 1:
                continue
            off = sub * ksub - t * qstrip
            diag_tiles.append((sub, ci, off if off + ksub - 1 > 0 else None))
    carries = run_tiles(qi, pl.multiple_of(qi * tq, tq), carries, diag_tiles)

    for ci, (h, t) in enumerate(chains):
        acc = acc_ref[ci]
        o = (acc[:MLA_V] / acc[MLA_V:MLA_V + 1]).T
        o_ref[t * qstrip:(t + 1) * qstrip, h * MLA_V:(h + 1) * MLA_V] = o.astype(o_ref.dtype)


def mla_attention(q_all, q_pe, kv_all, k_pe, batch, seq, tq=1024):
    n = q_all.shape[0]
    tq = min(tq, seq)
    qstrip = min(ATTN_QUERIES_PER_TILE, tq)
    nq = seq // tq
    pair = 2 * MLA_NOPE
    n_pairs = MLA_HEADS // 2
    return pl.pallas_call(
        functools.partial(_mla_attn_kernel, tq=tq),
        grid=(batch, n_pairs, nq),
        in_specs=[
            pl.BlockSpec((tq, pair), lambda b, hp, i: (b * nq + i, hp)),
            pl.BlockSpec((tq, LANES), lambda b, hp, i: (b * nq + i, hp)),
            pl.BlockSpec((seq, pair), lambda b, hp, i: (b, hp)),
            pl.BlockSpec((seq, pair), lambda b, hp, i: (b, n_pairs + hp)),
            pl.BlockSpec((seq, LANES), lambda b, hp, i: (b, 0)),
        ],
        out_specs=pl.BlockSpec((tq, pair), lambda b, hp, i: (b * nq + i, hp)),
        out_shape=jax.ShapeDtypeStruct((n, MLA_HEADS * MLA_V), BF16),
        scratch_shapes=[pltpu.VMEM((2, seq, 2 * LANES), BF16),
                        pltpu.VMEM((2, seq // tq, MLA_V + ATTN_ONES_ROWS, tq), BF16),
                        pltpu.VMEM((2, 2 * LANES, tq), BF16),
                        pltpu.VMEM((2 * (tq // qstrip), MLA_V + ATTN_ONES_ROWS, qstrip), F32)],
        compiler_params=_params("parallel", "parallel", "arbitrary"),
        name="mla_attention",
    )(q_all, q_pe, kv_all, kv_all, k_pe)


def _xattn_kernel(h_ref, x_ref, wq_ref, k_ref, v_ref, wo_ref, gpost_ref, gpre_ref, hout_ref, xn_ref):
    q = jnp.dot(x_ref[...], wq_ref[...], preferred_element_type=F32).astype(BF16)
    heads = []
    for h in range(XA_HEADS):
        sl = slice(h * XA_DIM, (h + 1) * XA_DIM)
        s = lax.dot_general(q[:, sl], k_ref[:, sl], (((1,), (1,)), ((), ())),
                            preferred_element_type=F32) * (XA_DIM ** -0.5)
        p = jnp.exp(s - jnp.max(s, axis=1, keepdims=True))
        p = (p / jnp.sum(p, axis=1, keepdims=True)).astype(BF16)
        heads.append(jnp.dot(p, v_ref[:, sl], preferred_element_type=F32).astype(BF16))
    o = jnp.concatenate(heads, axis=1)
    t = jnp.dot(o, wo_ref[...], preferred_element_type=F32)
    h = h_ref[...] + _rms(t, gpost_ref[...])
    hout_ref[...] = h
    xn_ref[...] = _rms(h, gpre_ref[...]).astype(xn_ref.dtype)


def cross_attention(h, xn, wq, k, v, wo, g_post, g_pre, seq, mem_len, tm=256):
    n, d = xn.shape
    xa = wq.shape[1]
    tm = min(tm, seq)
    steps_per_batch = seq // tm
    row = pl.BlockSpec((tm, d), lambda i: (i, 0))
    vec = pl.BlockSpec((1, d), lambda i: (0, 0))
    mem_blk = pl.BlockSpec((mem_len, xa), lambda i: (i // steps_per_batch, 0))
    return pl.pallas_call(
        _xattn_kernel, grid=(n // tm,),
        in_specs=[row, row,
                  pl.BlockSpec((d, xa), lambda i: (0, 0)),
                  mem_blk, mem_blk,
                  pl.BlockSpec((xa, d), lambda i: (0, 0)),
                  vec, vec],
        out_specs=[row, row],
        out_shape=[jax.ShapeDtypeStruct((n, d), F32), jax.ShapeDtypeStruct((n, d), BF16)],
        compiler_params=_params("parallel"), name="cross_attention",
    )(h, xn, wq, k, v, wo, g_post.reshape(1, d).astype(F32), g_pre.reshape(1, d).astype(F32))


S5_ROWS_PER_CHUNK = 32


def _cmul(ar, ai, br, bi):
    return ar * br - ai * bi, ar * bi + ai * br


def _s5_kernel(u_ref, perm_ref, unperm_ref, lre_ref, lim_ref, ldt_ref, bre_ref, bim_ref,
               cre_ref, cim_ref, d_ref, y_ref,
               wbr_ref, wbi_ref, pwr_ref, pwi_ref, xr_ref, xi_ref, sr_ref, si_ref,
               zr_ref, zi_ref, pr_ref, pi_ref,
               car_ref, cai_ref, *, tt):
    t = pl.program_id(2)
    nlb, rr, nchunk = xr_ref.shape[0], xr_ref.shape[1], xr_ref.shape[2]
    lanes = [slice(lb * LANES, (lb + 1) * LANES) for lb in range(nlb)]

    @pl.when(t == 0)
    def _():
        lre, lim = lre_ref[...], lim_ref[...]
        dt = jnp.exp(ldt_ref[...])
        mag = jnp.exp(lre * dt)
        a_re, a_im = mag * jnp.cos(lim * dt), mag * jnp.sin(lim * dt)
        den = lre * lre + lim * lim
        nr, ni = a_re - 1.0, a_im
        cr = (nr * lre + ni * lim) / den
        ci = (ni * lre - nr * lim) / den
        for lb in range(nlb):
            bre, bim = bre_ref[:, lanes[lb]], bim_ref[:, lanes[lb]]
            wbr_ref[:, lanes[lb]] = (cr[lb] * bre - ci[lb] * bim).astype(BF16)
            wbi_ref[:, lanes[lb]] = (cr[lb] * bim + ci[lb] * bre).astype(BF16)
        p_re, p_im = a_re, a_im
        for r in range(rr):
            pwr_ref[r] = p_re
            pwi_ref[r] = p_im
            p_re, p_im = _cmul(p_re, p_im, a_re, a_im)
        car_ref[...] = jnp.zeros(car_ref.shape, F32)
        cai_ref[...] = jnp.zeros(cai_ref.shape, F32)

    u = jnp.dot(perm_ref[...], u_ref[...], preferred_element_type=F32).astype(BF16)
    bu_re = jnp.dot(u, wbr_ref[...], preferred_element_type=F32)
    bu_im = jnp.dot(u, wbi_ref[...], preferred_element_type=F32)
    for lb in range(nlb):
        xr_ref[lb] = bu_re[:, lanes[lb]].reshape(rr, nchunk, LANES)
        xi_ref[lb] = bu_im[:, lanes[lb]].reshape(rr, nchunk, LANES)

    for lb in range(nlb):
        ar, ai = pwr_ref[0, lb], pwi_ref[0, lb]
        pr, pi = xr_ref[lb, 0], xi_ref[lb, 0]
        for r in range(1, rr):
            mr, mi = _cmul(ar, ai, pr, pi)
            pr = mr + xr_ref[lb, r]
            pi = mi + xi_ref[lb, r]
            xr_ref[lb, r] = pr
            xi_ref[lb, r] = pi
        zr_ref[lb] = pr
        zi_ref[lb] = pi

    ac_re, ac_im = pwr_ref[rr - 1], pwi_ref[rr - 1]

    def chunk_step(c, carry):
        cr, ci = carry
        pr_ref[:, pl.ds(c, 1), :] = cr
        pi_ref[:, pl.ds(c, 1), :] = ci
        mr, mi = _cmul(ac_re, ac_im, cr, ci)
        return mr + zr_ref[:, pl.ds(c, 1), :], mi + zi_ref[:, pl.ds(c, 1), :]

    cr, ci = lax.fori_loop(0, nchunk, chunk_step, (car_ref[...], cai_ref[...]), unroll=True)
    car_ref[...] = cr
    cai_ref[...] = ci

    for lb in range(nlb):
        er, ei = pr_ref[lb], pi_ref[lb]
        for r in range(rr):
            mr, mi = _cmul(pwr_ref[r, lb], pwi_ref[r, lb], er, ei)
            rows = slice(r * nchunk, (r + 1) * nchunk)
            sr_ref[rows, lanes[lb]] = (xr_ref[lb, r] + mr).astype(BF16)
            si_ref[rows, lanes[lb]] = (xi_ref[lb, r] + mi).astype(BF16)

    y = (jnp.dot(sr_ref[...], cre_ref[...], preferred_element_type=F32)
         - jnp.dot(si_ref[...], cim_ref[...], preferred_element_type=F32))
    y = (y + d_ref[...] * u.astype(F32)).astype(BF16)
    y_ref[...] = jnp.dot(unperm_ref[...], y, preferred_element_type=F32).astype(y_ref.dtype)


def s5_scan(u, lam_re, lam_im, log_dt, b_re, b_im, c_re, c_im, d_skip, batch, seq, tt=512):
    n, d = u.shape
    groups, state = lam_re.shape
    gb = S5_GROUPS_PER_BLOCK
    nblk = groups // gb
    width = gb * state
    cols = gb * S5_GROUP
    tt = min(tt, seq)
    nt = seq // tt
    eye = jnp.eye(gb, dtype=F32)

    def block_diag_b(b):
        b4 = b.astype(F32).reshape(nblk, gb, state, S5_GROUP).transpose(0, 1, 3, 2)
        return (b4[:, :, :, None, :] * eye[None, :, None, :, None]).reshape(nblk * cols, width)

    def block_diag_c(c):
        c4 = c.astype(F32).reshape(nblk, gb, S5_GROUP, state).transpose(0, 1, 3, 2)
        return (c4[:, :, :, None, :] * eye[None, :, None, :, None]).reshape(nblk * width, cols).astype(BF16)

    nlb = width // LANES
    flat = lambda a: a.astype(F32).reshape(nblk * nlb, 1, LANES)
    ldt = flat(jnp.broadcast_to(log_dt.astype(F32)[:, None], (groups, state)))
    vec = pl.BlockSpec((nlb, 1, LANES), lambda j, b, t: (j, 0, 0))
    rr = min(S5_ROWS_PER_CHUNK, tt)
    nchunk = tt // rr
    p_idx = jnp.arange(tt)
    perm = jax.nn.one_hot((p_idx % nchunk) * rr + p_idx // nchunk, tt, dtype=BF16)
    square = pl.BlockSpec((tt, tt), lambda j, b, t: (0, 0))
    return pl.pallas_call(
        functools.partial(_s5_kernel, tt=tt),
        grid=(nblk, batch, nt),
        in_specs=[pl.BlockSpec((tt, cols), lambda j, b, t: (b * nt + t, j)),
                  square, square,
                  vec, vec, vec,
                  pl.BlockSpec((cols, width), lambda j, b, t: (j, 0)),
                  pl.BlockSpec((cols, width), lambda j, b, t: (j, 0)),
                  pl.BlockSpec((width, cols), lambda j, b, t: (j, 0)),
                  pl.BlockSpec((width, cols), lambda j, b, t: (j, 0)),
                  pl.BlockSpec((1, cols), lambda j, b, t: (0, j))],
        out_specs=pl.BlockSpec((tt, cols), lambda j, b, t: (b * nt + t, j)),
        out_shape=jax.ShapeDtypeStruct((n, d), BF16),
        scratch_shapes=[pltpu.VMEM((cols, width), BF16), pltpu.VMEM((cols, width), BF16),
                        pltpu.VMEM((rr, nlb, 1, LANES), F32), pltpu.VMEM((rr, nlb, 1, LANES), F32),
                        pltpu.VMEM((nlb, rr, nchunk, LANES), F32), pltpu.VMEM((nlb, rr, nchunk, LANES), F32),
                        pltpu.VMEM((tt, width), BF16), pltpu.VMEM((tt, width), BF16),
                        pltpu.VMEM((nlb, nchunk, LANES), F32), pltpu.VMEM((nlb, nchunk, LANES), F32),
                        pltpu.VMEM((nlb, nchunk, LANES), F32), pltpu.VMEM((nlb, nchunk, LANES), F32),
                        pltpu.VMEM((nlb, 1, LANES), F32), pltpu.VMEM((nlb, 1, LANES), F32)],
        compiler_params=_params("parallel", "arbitrary", "arbitrary"),
        name="s5_scan",
    )(u, perm, perm.T, flat(lam_re), flat(lam_im), ldt, block_diag_b(b_re), block_diag_b(b_im),
      block_diag_c(c_re), block_diag_c(c_im), d_skip.astype(F32).reshape(1, d))


def _retention_kernel(q_ref, k_ref, v_ref, g_ref, cos_ref, sin_ref, gn_ref, lg_ref, o_ref,
                      state_ref, *, chunks):
    c = RET_CHUNK
    half = q_ref.shape[1] // 2

    @pl.when(pl.program_id(2) == 0)
    def _():
        state_ref[...] = jnp.zeros(state_ref.shape, F32)

    lg = lg_ref[0][:1, :1]
    ri = lax.broadcasted_iota(jnp.int32, (c, c), 0)
    ci = lax.broadcasted_iota(jnp.int32, (c, c), 1)
    rel = (ri - ci).astype(F32)
    decay = jnp.where(rel >= 0, jnp.exp(lg * jnp.maximum(rel, 0.0)), 0.0)
    idx = lax.broadcasted_iota(jnp.int32, (c, 1), 0).astype(F32)
    q_decay = jnp.exp(lg * (idx + 1.0))
    k_decay = jnp.exp(lg * (c - 1.0 - idx))
    chunk_decay = jnp.exp(lg * c)
    k_scale = q_ref.shape[1] ** -0.5

    def rope(x, cos, sin):
        x1, x2 = x[:, :half], x[:, half:]
        return jnp.concatenate([x1 * cos - x2 * sin, x2 * cos + x1 * sin], axis=1)

    for ch in range(chunks):
        rows = slice(ch * c, (ch + 1) * c)
        cos, sin = cos_ref[rows, :], sin_ref[rows, :]
        q = rope(q_ref[rows, :].astype(F32), cos, sin)
        k = rope(k_ref[rows, :].astype(F32), cos, sin) * k_scale
        v = v_ref[rows, :]
        qb = q.astype(BF16)
        inner = lax.dot_general(qb, k.astype(BF16), (((1,), (1,)), ((), ())),
                                preferred_element_type=F32) * decay
        y = jnp.dot(inner.astype(BF16), v, preferred_element_type=F32)
        state = state_ref[...]
        y = y + jnp.dot(qb, state.astype(BF16), preferred_element_type=F32) * q_decay
        kd = (k * k_decay).astype(BF16)
        state_ref[...] = state * chunk_decay + lax.dot_general(
            kd, v, (((0,), (0,)), ((), ())), preferred_element_type=F32)
        mu = jnp.mean(y, axis=1, keepdims=True)
        yc = y - mu
        var = jnp.mean(yc * yc, axis=1, keepdims=True)
        yn = yc * lax.rsqrt(var + NORM_EPS) * gn_ref[...]
        o_ref[rows, :] = (g_ref[rows, :].astype(F32) * yn).astype(o_ref.dtype)


def retention(qk, v, g, cos, sin, gn, log_gamma_tiles, batch, seq, rows_per_step=512):
    n = qk.shape[0]
    dqk = qk.shape[1] // (2 * RET_HEADS)
    dv = v.shape[1] // RET_HEADS
    rows = min(rows_per_step, seq)
    ns = seq // rows
    row_qk = lambda off: pl.BlockSpec((rows, dqk), lambda b, h, i: (b * ns + i, off + h))
    row_v = pl.BlockSpec((rows, dv), lambda b, h, i: (b * ns + i, h))
    row_t = pl.BlockSpec((rows, LANES), lambda b, h, i: (b * ns + i, 0))
    return pl.pallas_call(
        functools.partial(_retention_kernel, chunks=rows // RET_CHUNK),
        grid=(batch, RET_HEADS, ns),
        in_specs=[row_qk(0), row_qk(RET_HEADS), row_v, row_v, row_t, row_t,
                  pl.BlockSpec((1, dv), lambda b, h, i: (0, h)),
                  pl.BlockSpec((1, 8, LANES), lambda b, h, i: (h, 0, 0))],
        out_specs=row_v,
        out_shape=jax.ShapeDtypeStruct((n, RET_HEADS * dv), BF16),
        scratch_shapes=[pltpu.VMEM((dqk, dv), F32)],
        compiler_params=_params("parallel", "parallel", "arbitrary"),
        name="retention",
    )(qk, qk, v, g, cos, sin, gn.astype(F32).reshape(1, -1), log_gamma_tiles)


def _mla_mixer(tn, rope_cs, wq_a, q_norm, wq_b, wkv_a, kv_norm, wkv_b, wo, batch, seq):
    kv_rank = kv_norm.shape[0]
    qk_dim = MLA_NOPE + MLA_ROPE
    wq_b3 = wq_b.reshape(-1, MLA_HEADS, qk_dim)
    wq_b_perm = jnp.concatenate([wq_b3[:, :, :MLA_NOPE].reshape(-1, MLA_HEADS * MLA_NOPE),
                                 wq_b3[:, :, MLA_NOPE:].reshape(-1, MLA_HEADS * MLA_ROPE)], axis=1)
    wkv_b3 = wkv_b.reshape(-1, MLA_HEADS, MLA_NOPE + MLA_V)
    wkv_b_perm = jnp.concatenate([wkv_b3[:, :, :MLA_NOPE].reshape(-1, MLA_HEADS * MLA_NOPE),
                                  wkv_b3[:, :, MLA_NOPE:].reshape(-1, MLA_HEADS * MLA_V)], axis=1)
    w_kpe = wkv_a[:, kv_rank:]
    w_kpe2 = jnp.concatenate([w_kpe, w_kpe], axis=1)

    c_q = matmul(tn, wq_a.astype(BF16), out_dtype=F32)
    c_q = rmsnorm(c_q, q_norm)
    q_all = matmul(c_q, wq_b_perm.astype(BF16), out_dtype=BF16, scale=qk_dim ** -0.5 * math.log2(math.e))
    c_kv = matmul(tn, wkv_a[:, :kv_rank].astype(BF16), out_dtype=F32)
    c_kv = rmsnorm(c_kv, kv_norm)
    kv_all = matmul(c_kv, wkv_b_perm.astype(BF16), out_dtype=BF16)
    k_pe = matmul(tn, w_kpe2.astype(BF16), out_dtype=F32)

    cos, sin = rope_cs
    nope_blocks = MLA_HEADS * MLA_NOPE // LANES
    q_pe = rope_pairs(q_all, nope_blocks, MLA_HEADS * MLA_ROPE // LANES, cos, sin)
    k_pe = rope_pairs(k_pe, 0, 1, cos, sin)
    o = mla_attention(q_all, q_pe, kv_all, k_pe, batch, seq)
    return matmul(o, wo.astype(BF16), out_dtype=F32)


def _s5_mixer(tn, lam_re, lam_im, log_dt, b_re, b_im, c_re, c_im, d_skip, w_glu, batch, seq):
    d = tn.shape[1]
    y = s5_scan(tn, lam_re, lam_im, log_dt, b_re, b_im, c_re, c_im, d_skip, batch, seq)
    glu_tn = 512
    return matmul(y, w_glu.astype(BF16), gate_col_block=d // glu_tn, out_dtype=F32, tn=glu_tn)


def _retention_mixer(tn, rope_cs, wq, wk, wv, wg, gn, wo, log_gamma_tiles, batch, seq):
    cos, sin = rope_cs
    qk = matmul(tn, jnp.concatenate([wq, wk], axis=1).astype(BF16), out_dtype=BF16)
    v = matmul(tn, wv.astype(BF16), out_dtype=BF16)
    g = matmul(tn, wg.astype(BF16), out_dtype=BF16, act="silu")
    y = retention(qk, v, g, cos, sin, gn, log_gamma_tiles, batch, seq)
    return matmul(y, wo.astype(BF16), out_dtype=F32)


def _rope_freqs(dim):
    return 1.0 / (ROPE_BASE ** (jnp.arange(0, dim, 2, dtype=F32) / dim))


def kernel(x, mem, positions, norm_gain, mem_norm, xa_wq, xa_wk, xa_wv, xa_wo, mlp_w1, mlp_w2,
           mla_wq_a, mla_q_norm, mla_wq_b, mla_wkv_a, mla_kv_norm, mla_wkv_b, mla_wo,
           s5_lam_re, s5_lam_im, s5_log_dt, s5_b_re, s5_b_im, s5_c_re, s5_c_im, s5_d, s5_w_glu,
           ret_wq, ret_wk, ret_wv, ret_wg, ret_gn, ret_wo):
    batch, seq, d = x.shape
    n = batch * seq
    depth = norm_gain.shape[0]
    mem_len = mem.shape[1]
    n_mixers = 3

    pos = positions.reshape(n)
    f_mla = jnp.tile(_rope_freqs(MLA_ROPE), 4)
    sign_mla = jnp.tile(jnp.concatenate([-jnp.ones(MLA_ROPE // 2, F32), jnp.ones(MLA_ROPE // 2, F32)]), 2)
    mla_cs = rope_tables(pos, f_mla, sign_mla)
    ret_cs = rope_tables(pos, _rope_freqs(d // RET_HEADS), jnp.ones(LANES, F32))
    log_gamma = jnp.log1p(-jnp.exp2(-5.0 - jnp.arange(RET_HEADS, dtype=F32)))
    log_gamma_tiles = jnp.broadcast_to(log_gamma[:, None, None], (RET_HEADS, 8, LANES))

    mem_n = rmsnorm(mem.reshape(batch * mem_len, d), mem_norm)
    mlp_w1_b, mlp_w2_b = mlp_w1.astype(BF16), mlp_w2.astype(BF16)

    h = x.reshape(n, d)
    tn = rmsnorm(h, norm_gain[0, 0])
    for i in range(depth):
        kind, slot = i % n_mixers, i // n_mixers
        g = norm_gain[i]
        if kind == 0:
            t = _mla_mixer(tn, mla_cs, mla_wq_a[slot], mla_q_norm[slot], mla_wq_b[slot],
                           mla_wkv_a[slot], mla_kv_norm[slot], mla_wkv_b[slot], mla_wo[slot],
                           batch, seq)
        elif kind == 1:
            t = _s5_mixer(tn, s5_lam_re[slot], s5_lam_im[slot], s5_log_dt[slot], s5_b_re[slot],
                          s5_b_im[slot], s5_c_re[slot], s5_c_im[slot], s5_d[slot], s5_w_glu[slot],
                          batch, seq)
        else:
            t = _retention_mixer(tn, ret_cs, ret_wq[slot], ret_wk[slot], ret_wv[slot], ret_wg[slot],
                                 ret_gn[slot], ret_wo[slot], log_gamma_tiles, batch, seq)
        h, tn = residual_norm(h, t, g[1], g[2])

        k_mem = matmul(mem_n, xa_wk[i].astype(BF16), out_dtype=BF16)
        v_mem = matmul(mem_n, xa_wv[i].astype(BF16), out_dtype=BF16)
        h, tn = cross_attention(h, tn, xa_wq[i].astype(BF16), k_mem, v_mem, xa_wo[i].astype(BF16),
                                g[3], g[4], seq, mem_len)

        hid = matmul(tn, mlp_w1_b, layer=i, out_dtype=BF16, act="relu2")
        t = matmul(hid, mlp_w2_b, layer=i, out_dtype=F32)
        if i + 1 < depth:
            h, tn = residual_norm(h, t, g[5], norm_gain[i + 1, 0])
        else:
            h = residual_norm(h, t, g[5])
    return h.reshape(batch, seq, d)
```

```python
import functools
import math

import jax
import jax.numpy as jnp
from jax import lax
from jax.experimental import pallas as pl
from jax.experimental.pallas import tpu as pltpu

F32 = jnp.float32
BF16 = jnp.bfloat16

V7X_VMEM_LIMIT_BYTES = 56 * 1024 * 1024
LANES = 128

MLA_HEADS = 64
MLA_NOPE = 128
MLA_ROPE = 64
MLA_V = 128
S5_GROUP = 16
S5_STATE = 64
S5_GROUPS_PER_BLOCK = 16
RET_HEADS = 16
RET_CHUNK = 128
XA_HEADS = 4
XA_DIM = 128
ROPE_BASE = 10000.0
NORM_EPS = 1e-6


def _params(*sem):
    return pltpu.CompilerParams(dimension_semantics=sem,
                                vmem_limit_bytes=V7X_VMEM_LIMIT_BYTES)


def _rms(x, g):
    return x * lax.rsqrt(jnp.mean(x * x, axis=-1, keepdims=True) + NORM_EPS) * g


def _rmsnorm_kernel(x_ref, g_ref, o_ref):
    o_ref[...] = _rms(x_ref[...].astype(F32), g_ref[...]).astype(o_ref.dtype)


def rmsnorm(x, g, out_dtype=BF16, tm=512):
    m, d = x.shape
    tm = min(tm, m)
    return pl.pallas_call(
        _rmsnorm_kernel,
        grid=(m // tm,),
        in_specs=[pl.BlockSpec((tm, d), lambda i: (i, 0)),
                  pl.BlockSpec((1, d), lambda i: (0, 0))],
        out_specs=pl.BlockSpec((tm, d), lambda i: (i, 0)),
        out_shape=jax.ShapeDtypeStruct((m, d), out_dtype),
        compiler_params=_params("parallel"),
        name="rmsnorm",
    )(x, g.reshape(1, d).astype(F32))


def _residual_kernel(h_ref, t_ref, gpost_ref, gpre_ref, hout_ref, xn_ref):
    h = h_ref[...] + _rms(t_ref[...].astype(F32), gpost_ref[...])
    hout_ref[...] = h
    xn_ref[...] = _rms(h, gpre_ref[...]).astype(xn_ref.dtype)


def _residual_last_kernel(h_ref, t_ref, gpost_ref, hout_ref):
    hout_ref[...] = h_ref[...] + _rms(t_ref[...].astype(F32), gpost_ref[...])


def residual_norm(h, t, g_post, g_pre=None, tm=256):
    m, d = h.shape
    tm = min(tm, m)
    row = pl.BlockSpec((tm, d), lambda i: (i, 0))
    vec = pl.BlockSpec((1, d), lambda i: (0, 0))
    if g_pre is None:
        return pl.pallas_call(
            _residual_last_kernel, grid=(m // tm,),
            in_specs=[row, row, vec], out_specs=row,
            out_shape=jax.ShapeDtypeStruct((m, d), F32),
            compiler_params=_params("parallel"), name="residual_last",
        )(h, t, g_post.reshape(1, d).astype(F32))
    return pl.pallas_call(
        _residual_kernel, grid=(m // tm,),
        in_specs=[row, row, vec, vec], out_specs=[row, row],
        out_shape=[jax.ShapeDtypeStruct((m, d), F32), jax.ShapeDtypeStruct((m, d), BF16)],
        compiler_params=_params("parallel"), name="residual_norm",
    )(h, t, g_post.reshape(1, d).astype(F32), g_pre.reshape(1, d).astype(F32))


def _activation(y, act):
    if act == "relu2":
        r = jnp.maximum(y, 0.0)
        return r * r
    if act == "silu":
        return y * jax.nn.sigmoid(y)
    assert act is None
    return y


MXU_TILE_COLS = 256


def _mm_kernel(*refs, nk, act, scale, gated, side_cast):
    n_in = 2 + gated + side_cast
    x_ref, w_ref = refs[0], refs[1]
    wg_ref = refs[2] if gated else None
    o_ref = refs[n_in]
    if side_cast:
        refs[n_in + 1][...] = refs[n_in - 1][...].astype(BF16)

    if nk == 1:
        x = x_ref[...]
        y = jnp.dot(x, w_ref[...], preferred_element_type=F32)
        if scale != 1.0:
            y = y * scale
        if gated:
            y = y * jax.nn.sigmoid(jnp.dot(x, wg_ref[...], preferred_element_type=F32))
        o_ref[...] = _activation(y, act).astype(o_ref.dtype)
        return

    k = pl.program_id(2)
    chunks = [slice(c, c + MXU_TILE_COLS) for c in range(0, o_ref.shape[1], MXU_TILE_COLS)]

    @pl.when(k == 0)
    def _():
        for cols in chunks:
            o_ref[:, cols] = jnp.dot(x_ref[...], w_ref[:, cols], preferred_element_type=F32)

    @pl.when(k > 0)
    def _():
        for cols in chunks:
            o_ref[:, cols] += jnp.dot(x_ref[...], w_ref[:, cols], preferred_element_type=F32)


SIDE_CAST_COLS = 1024


def matmul(x, w, *, out_dtype, act=None, scale=1.0, gate_col_block=None,
           cast_stack=None, cast_layer=None, tm=1024, tn=1024, tk=4096):
    m, kdim = x.shape
    gated = gate_col_block is not None
    n = w.shape[-1] // 2 if gated else w.shape[-1]
    tm, tn, tk = min(tm, m), min(tn, n), min(tk, kdim)
    assert m % tm == 0 and n % tn == 0 and kdim % tk == 0
    nk = kdim // tk
    if nk > 1:
        assert not gated and act is None and scale == 1.0 and out_dtype == F32
        assert tn % MXU_TILE_COLS == 0

    def w_spec(col0):
        return pl.BlockSpec((tk, tn), lambda i, j, k: (k, col0 + j))

    operands = [x, w] + ([w] if gated else [])
    in_specs = [pl.BlockSpec((tm, tk), lambda i, j, k: (i, k)), w_spec(0)]
    if gated:
        in_specs.append(w_spec(gate_col_block))
    out_specs = [pl.BlockSpec((tm, tn), lambda i, j, k: (i, j))]
    out_shape = [jax.ShapeDtypeStruct((m, n), out_dtype)]
    gm, gn = m // tm, n // tn
    side_cast = cast_stack is not None
    if side_cast:
        layer_shape = cast_stack.shape[1:]
        steps = gm * gn * nk
        rows = math.prod(layer_shape) // (steps * SIDE_CAST_COLS)
        assert rows * steps * SIDE_CAST_COLS == math.prod(layer_shape) and rows % 16 == 0
        operands.append(cast_stack.reshape(-1, SIDE_CAST_COLS))
        in_specs.append(pl.BlockSpec(
            (rows, SIDE_CAST_COLS), lambda i, j, k: (cast_layer * steps + (i * gn + j) * nk + k, 0)))
        out_specs.append(pl.BlockSpec((rows, SIDE_CAST_COLS), lambda i, j, k: ((i * gn + j) * nk + k, 0)))
        out_shape.append(jax.ShapeDtypeStruct((rows * steps, SIDE_CAST_COLS), BF16))
    res = pl.pallas_call(
        functools.partial(_mm_kernel, nk=nk, act=act, scale=scale, gated=gated, side_cast=side_cast),
        grid=(gm, gn, nk),
        in_specs=in_specs,
        out_specs=out_specs,
        out_shape=out_shape,
        compiler_params=_params("parallel", "parallel", "arbitrary"),
        name="matmul",
    )(*operands)
    if side_cast:
        return res[0], res[1].reshape(layer_shape)
    return res[0]


def _rope_table_kernel(pos_ref, freq_ref, sign_ref, cos_ref, sin_ref):
    ang = pos_ref[...].astype(F32) * freq_ref[...]
    cos_ref[...] = jnp.cos(ang)
    sin_ref[...] = jnp.sin(ang) * sign_ref[...]


def rope_tables(pos, inv_freq_lanes, sign_lanes, tm=1024):
    n = pos.shape[0]
    tm = min(tm, n)
    out = jax.ShapeDtypeStruct((n, LANES), F32)
    return pl.pallas_call(
        _rope_table_kernel, grid=(n // tm,),
        in_specs=[pl.BlockSpec((tm, 1), lambda i: (i, 0)),
                  pl.BlockSpec((1, LANES), lambda i: (0, 0)),
                  pl.BlockSpec((1, LANES), lambda i: (0, 0))],
        out_specs=[pl.BlockSpec((tm, LANES), lambda i: (i, 0))] * 2,
        out_shape=[out, out],
        compiler_params=_params("parallel"), name="rope_tables",
    )(pos.reshape(n, 1), inv_freq_lanes.reshape(1, LANES), sign_lanes.reshape(1, LANES))


def _swap_half_pairs(x):
    lane = lax.broadcasted_iota(jnp.int32, x.shape, 1)
    return jnp.where(lane % MLA_ROPE < MLA_ROPE // 2,
                     pltpu.roll(x, LANES - MLA_ROPE // 2, 1),
                     pltpu.roll(x, MLA_ROPE // 2, 1))


def _rope_pairs_kernel(x_ref, cos_ref, sin_ref, o_ref):
    x = x_ref[...].astype(F32)
    o_ref[...] = (x * cos_ref[...] + _swap_half_pairs(x) * sin_ref[...]).astype(o_ref.dtype)


def rope_pairs(x, col_block0, n_col_blocks, cos, sin, tm=1024):
    n = x.shape[0]
    tm = min(tm, n)
    return pl.pallas_call(
        _rope_pairs_kernel, grid=(n // tm, n_col_blocks),
        in_specs=[pl.BlockSpec((tm, LANES), lambda i, j: (i, col_block0 + j)),
                  pl.BlockSpec((tm, LANES), lambda i, j: (i, 0)),
                  pl.BlockSpec((tm, LANES), lambda i, j: (i, 0))],
        out_specs=pl.BlockSpec((tm, LANES), lambda i, j: (i, j)),
        out_shape=jax.ShapeDtypeStruct((n, n_col_blocks * LANES), BF16),
        compiler_params=_params("parallel", "arbitrary"), name="rope_pairs",
    )(x, cos, sin)


ATTN_KEYS_PER_TILE = 256
ATTN_QUERIES_PER_TILE = 256
ATTN_SCORE_LOOKAHEAD = 8
ATTN_ONES_ROWS = 16


def _mla_attn_kernel(qn_ref, qp_ref, cos_ref, sin_ref, kn_ref, v_ref, kp_ref, o_ref,
                     kcat_ref, vt_ref, qt_ref, acc_ref, *, tq):
    ksub, qstrip = min(ATTN_KEYS_PER_TILE, tq), min(ATTN_QUERIES_PER_TILE, tq)
    qi = pl.program_id(2)
    nblk = kn_ref.shape[0] // tq
    nsub = tq // ksub
    chains = [(h, t) for h in range(2) for t in range(tq // qstrip)]

    @pl.when(qi == 0)
    def _():
        lane = lax.broadcasted_iota(jnp.int32, (tq, LANES), 1)
        for c in range(nblk):
            rows = slice(c * tq, (c + 1) * tq)
            kp = kp_ref[rows, :]
            for h in range(2):
                kcat_ref[h, rows, :MLA_NOPE] = kn_ref[rows, h * MLA_NOPE:(h + 1) * MLA_NOPE]
                keep = (lane < MLA_ROPE) if h == 0 else (lane >= MLA_ROPE)
                kcat_ref[h, rows, MLA_NOPE:] = jnp.where(keep, kp, jnp.zeros_like(kp))
                vt_ref[h, c, :MLA_V] = v_ref[rows, h * MLA_V:(h + 1) * MLA_V].T
                vt_ref[h, c, MLA_V:] = jnp.ones((ATTN_ONES_ROWS, tq), BF16)

    lane_q = lax.broadcasted_iota(jnp.int32, (tq, LANES), 1)
    qp = qp_ref[...].astype(F32)
    qp = (qp * cos_ref[...] + _swap_half_pairs(qp) * sin_ref[...]).astype(BF16)
    for h in range(2):
        keep = (lane_q < MLA_ROPE) if h == 0 else (lane_q >= MLA_ROPE)
        qcat = jnp.concatenate(
            [qn_ref[:, h * MLA_NOPE:(h + 1) * MLA_NOPE], jnp.where(keep, qp, jnp.zeros_like(qp))],
            axis=1)
        qt_ref[h] = qcat.T
    acc_ref[...] = jnp.zeros(acc_ref.shape, F32)

    krow = lax.broadcasted_iota(jnp.int32, (ksub, qstrip), 0)
    qcol = lax.broadcasted_iota(jnp.int32, (ksub, qstrip), 1)

    def scores(ci, k, mask_offset):
        h, t = chains[ci]
        s = jnp.dot(k, qt_ref[h, :, t * qstrip:(t + 1) * qstrip], preferred_element_type=F32)
        if mask_offset is not None:
            s = jnp.where(krow + mask_offset <= qcol, s, -jnp.inf)
        return s

    def update(ci, carry, s, vt):
        m_prev = carry
        m_new = jnp.maximum(m_prev, jnp.max(s, axis=0, keepdims=True))
        alpha = jnp.exp2(m_prev - m_new)
        p = jnp.exp2(s - m_new)
        acc_ref[ci] = alpha * acc_ref[ci] + jnp.dot(vt, p.astype(BF16), preferred_element_type=F32)
        return m_new

    def run_tiles(jblk, start, carries, tiles):
        carries = list(carries)
        pending = []
        for idx in range(len(tiles) + ATTN_SCORE_LOOKAHEAD):
            if idx < len(tiles):
                sub, ci, off = tiles[idx]
                k = kcat_ref[chains[ci][0], pl.ds(start + sub * ksub, ksub), :]
                pending.append(scores(ci, k, off))
            if idx >= ATTN_SCORE_LOOKAHEAD:
                done = idx - ATTN_SCORE_LOOKAHEAD
                sub, ci, off = tiles[done]
                vt = vt_ref[chains[ci][0], jblk, :, sub * ksub:(sub + 1) * ksub]
                carries[ci] = update(ci, carries[ci], pending[done], vt)
                pending[done] = None
        return tuple(carries)

    full_tiles = [(sub, ci, None) for sub in range(nsub) for ci in range(len(chains))]

    def full_block(j, carries):
        return run_tiles(j, pl.multiple_of(j * tq, tq), carries, full_tiles)

    init = tuple(jnp.full((1, qstrip), -jnp.inf, F32) for _ in chains)
    carries = lax.fori_loop(0, qi, full_block, init)

    diag_tiles = []
    for sub in range(nsub):
        for ci, (h, t) in enumerate(chains):
            if sub * ksub > (t + 1) * qstrip - 1:
                continue
            off = sub * ksub - t * qstrip
            diag_tiles.append((sub, ci, off if off + ksub - 1 > 0 else None))
    carries = run_tiles(qi, pl.multiple_of(qi * tq, tq), carries, diag_tiles)

    for ci, (h, t) in enumerate(chains):
        acc = acc_ref[ci]
        o = (acc[:MLA_V] / acc[MLA_V:MLA_V + 1]).T
        o_ref[t * qstrip:(t + 1) * qstrip, h * MLA_V:(h + 1) * MLA_V] = o.astype(o_ref.dtype)


def mla_attention(q_all, kv_all, k_pe, cos, sin, batch, seq, tq=1024):
    n = q_all.shape[0]
    tq = min(tq, seq)
    qstrip = min(ATTN_QUERIES_PER_TILE, tq)
    nq = seq // tq
    pair = 2 * MLA_NOPE
    n_pairs = MLA_HEADS // 2
    rope_block0 = MLA_HEADS * MLA_NOPE // LANES
    return pl.pallas_call(
        functools.partial(_mla_attn_kernel, tq=tq),
        grid=(batch, n_pairs, nq),
        in_specs=[
            pl.BlockSpec((tq, pair), lambda b, hp, i: (b * nq + i, hp)),
            pl.BlockSpec((tq, LANES), lambda b, hp, i: (b * nq + i, rope_block0 + hp)),
            pl.BlockSpec((tq, LANES), lambda b, hp, i: (b * nq + i, 0)),
            pl.BlockSpec((tq, LANES), lambda b, hp, i: (b * nq + i, 0)),
            pl.BlockSpec((seq, pair), lambda b, hp, i: (b, hp)),
            pl.BlockSpec((seq, pair), lambda b, hp, i: (b, n_pairs + hp)),
            pl.BlockSpec((seq, LANES), lambda b, hp, i: (b, 0)),
        ],
        out_specs=pl.BlockSpec((tq, pair), lambda b, hp, i: (b * nq + i, hp)),
        out_shape=jax.ShapeDtypeStruct((n, MLA_HEADS * MLA_V), BF16),
        scratch_shapes=[pltpu.VMEM((2, seq, 2 * LANES), BF16),
                        pltpu.VMEM((2, seq // tq, MLA_V + ATTN_ONES_ROWS, tq), BF16),
                        pltpu.VMEM((2, 2 * LANES, tq), BF16),
                        pltpu.VMEM((2 * (tq // qstrip), MLA_V + ATTN_ONES_ROWS, qstrip), F32)],
        compiler_params=_params("parallel", "parallel", "arbitrary"),
        name="mla_attention",
    )(q_all, q_all, cos, sin, kv_all, kv_all, k_pe)


def _xattn_kernel(h_ref, wq_ref, k_ref, v_ref, wo_ref, gin_ref, gpost_ref, gpre_ref, hout_ref, xn_ref):
    x = _rms(h_ref[...], gin_ref[...]).astype(BF16)
    q = jnp.dot(x, wq_ref[...], preferred_element_type=F32).astype(BF16)
    heads = []
    for h in range(XA_HEADS):
        sl = slice(h * XA_DIM, (h + 1) * XA_DIM)
        s = lax.dot_general(q[:, sl], k_ref[:, sl], (((1,), (1,)), ((), ())),
                            preferred_element_type=F32) * (XA_DIM ** -0.5)
        p = jnp.exp(s - jnp.max(s, axis=1, keepdims=True))
        p = (p / jnp.sum(p, axis=1, keepdims=True)).astype(BF16)
        heads.append(jnp.dot(p, v_ref[:, sl], preferred_element_type=F32).astype(BF16))
    o = jnp.concatenate(heads, axis=1)
    t = jnp.dot(o, wo_ref[...], preferred_element_type=F32)
    h = h_ref[...] + _rms(t, gpost_ref[...])
    hout_ref[...] = h
    xn_ref[...] = _rms(h, gpre_ref[...]).astype(xn_ref.dtype)


def cross_attention(h, wq, k, v, wo, g_in, g_post, g_pre, seq, mem_len, tm=256):
    n, d = h.shape
    xa = wq.shape[1]
    tm = min(tm, seq)
    steps_per_batch = seq // tm
    row = pl.BlockSpec((tm, d), lambda i: (i, 0))
    vec = pl.BlockSpec((1, d), lambda i: (0, 0))
    mem_blk = pl.BlockSpec((mem_len, xa), lambda i: (i // steps_per_batch, 0))
    return pl.pallas_call(
        _xattn_kernel, grid=(n // tm,),
        in_specs=[row,
                  pl.BlockSpec((d, xa), lambda i: (0, 0)),
                  mem_blk, mem_blk,
                  pl.BlockSpec((xa, d), lambda i: (0, 0)),
                  vec, vec, vec],
        out_specs=[row, row],
        out_shape=[jax.ShapeDtypeStruct((n, d), F32), jax.ShapeDtypeStruct((n, d), BF16)],
        compiler_params=_params("parallel"), name="cross_attention",
    )(h, wq, k, v, wo, *(g.reshape(1, d).astype(F32) for g in (g_in, g_post, g_pre)))


S5_ROWS_PER_CHUNK = 32


def _cmul(ar, ai, br, bi):
    return ar * br - ai * bi, ar * bi + ai * br


def _s5_kernel(u_ref, perm_ref, unperm_ref, lre_ref, lim_ref, ldt_ref, bre_ref, bim_ref,
               cre_ref, cim_ref, d_ref, y_ref,
               wbr_ref, wbi_ref, pwr_ref, pwi_ref, xr_ref, xi_ref, sr_ref, si_ref,
               zr_ref, zi_ref, pr_ref, pi_ref,
               car_ref, cai_ref, *, tt):
    t = pl.program_id(2)
    nlb, rr, nchunk = xr_ref.shape[0], xr_ref.shape[1], xr_ref.shape[2]
    lanes = [slice(lb * LANES, (lb + 1) * LANES) for lb in range(nlb)]

    @pl.when(t == 0)
    def _():
        lre, lim = lre_ref[...], lim_ref[...]
        dt = jnp.exp(ldt_ref[...])
        mag = jnp.exp(lre * dt)
        a_re, a_im = mag * jnp.cos(lim * dt), mag * jnp.sin(lim * dt)
        den = lre * lre + lim * lim
        nr, ni = a_re - 1.0, a_im
        cr = (nr * lre + ni * lim) / den
        ci = (ni * lre - nr * lim) / den
        for lb in range(nlb):
            bre, bim = bre_ref[:, lanes[lb]], bim_ref[:, lanes[lb]]
            wbr_ref[:, lanes[lb]] = (cr[lb] * bre - ci[lb] * bim).astype(BF16)
            wbi_ref[:, lanes[lb]] = (cr[lb] * bim + ci[lb] * bre).astype(BF16)
        p_re, p_im = a_re, a_im
        for r in range(rr):
            pwr_ref[r] = p_re
            pwi_ref[r] = p_im
            p_re, p_im = _cmul(p_re, p_im, a_re, a_im)
        car_ref[...] = jnp.zeros(car_ref.shape, F32)
        cai_ref[...] = jnp.zeros(cai_ref.shape, F32)

    u = jnp.dot(perm_ref[...], u_ref[...], preferred_element_type=F32).astype(BF16)
    bu_re = jnp.dot(u, wbr_ref[...], preferred_element_type=F32)
    bu_im = jnp.dot(u, wbi_ref[...], preferred_element_type=F32)
    for lb in range(nlb):
        xr_ref[lb] = bu_re[:, lanes[lb]].reshape(rr, nchunk, LANES)
        xi_ref[lb] = bu_im[:, lanes[lb]].reshape(rr, nchunk, LANES)

    for lb in range(nlb):
        ar, ai = pwr_ref[0, lb], pwi_ref[0, lb]
        pr, pi = xr_ref[lb, 0], xi_ref[lb, 0]
        for r in range(1, rr):
            mr, mi = _cmul(ar, ai, pr, pi)
            pr = mr + xr_ref[lb, r]
            pi = mi + xi_ref[lb, r]
            xr_ref[lb, r] = pr
            xi_ref[lb, r] = pi
        zr_ref[lb] = pr
        zi_ref[lb] = pi

    ac_re, ac_im = pwr_ref[rr - 1], pwi_ref[rr - 1]

    def chunk_step(c, carry):
        cr, ci = carry
        pr_ref[:, pl.ds(c, 1), :] = cr
        pi_ref[:, pl.ds(c, 1), :] = ci
        mr, mi = _cmul(ac_re, ac_im, cr, ci)
        return mr + zr_ref[:, pl.ds(c, 1), :], mi + zi_ref[:, pl.ds(c, 1), :]

    cr, ci = lax.fori_loop(0, nchunk, chunk_step, (car_ref[...], cai_ref[...]), unroll=True)
    car_ref[...] = cr
    cai_ref[...] = ci

    for lb in range(nlb):
        er, ei = pr_ref[lb], pi_ref[lb]
        for r in range(rr):
            mr, mi = _cmul(pwr_ref[r, lb], pwi_ref[r, lb], er, ei)
            rows = slice(r * nchunk, (r + 1) * nchunk)
            sr_ref[rows, lanes[lb]] = (xr_ref[lb, r] + mr).astype(BF16)
            si_ref[rows, lanes[lb]] = (xi_ref[lb, r] + mi).astype(BF16)

    y = (jnp.dot(sr_ref[...], cre_ref[...], preferred_element_type=F32)
         - jnp.dot(si_ref[...], cim_ref[...], preferred_element_type=F32))
    y = (y + d_ref[...] * u.astype(F32)).astype(BF16)
    y_ref[...] = jnp.dot(unperm_ref[...], y, preferred_element_type=F32).astype(y_ref.dtype)


def s5_scan(u, lam_re, lam_im, log_dt, b_re, b_im, c_re, c_im, d_skip, batch, seq, tt=512):
    n, d = u.shape
    groups, state = lam_re.shape
    gb = S5_GROUPS_PER_BLOCK
    nblk = groups // gb
    width = gb * state
    cols = gb * S5_GROUP
    tt = min(tt, seq)
    nt = seq // tt
    eye = jnp.eye(gb, dtype=F32)

    def block_diag_b(b):
        b4 = b.astype(F32).reshape(nblk, gb, state, S5_GROUP).transpose(0, 1, 3, 2)
        return (b4[:, :, :, None, :] * eye[None, :, None, :, None]).reshape(nblk * cols, width)

    def block_diag_c(c):
        c4 = c.astype(F32).reshape(nblk, gb, S5_GROUP, state).transpose(0, 1, 3, 2)
        return (c4[:, :, :, None, :] * eye[None, :, None, :, None]).reshape(nblk * width, cols).astype(BF16)

    nlb = width // LANES
    flat = lambda a: a.astype(F32).reshape(nblk * nlb, 1, LANES)
    ldt = flat(jnp.broadcast_to(log_dt.astype(F32)[:, None], (groups, state)))
    vec = pl.BlockSpec((nlb, 1, LANES), lambda j, b, t: (j, 0, 0))
    rr = min(S5_ROWS_PER_CHUNK, tt)
    nchunk = tt // rr
    p_idx = jnp.arange(tt)
    perm = jax.nn.one_hot((p_idx % nchunk) * rr + p_idx // nchunk, tt, dtype=BF16)
    square = pl.BlockSpec((tt, tt), lambda j, b, t: (0, 0))
    return pl.pallas_call(
        functools.partial(_s5_kernel, tt=tt),
        grid=(nblk, batch, nt),
        in_specs=[pl.BlockSpec((tt, cols), lambda j, b, t: (b * nt + t, j)),
                  square, square,
                  vec, vec, vec,
                  pl.BlockSpec((cols, width), lambda j, b, t: (j, 0)),
                  pl.BlockSpec((cols, width), lambda j, b, t: (j, 0)),
                  pl.BlockSpec((width, cols), lambda j, b, t: (j, 0)),
                  pl.BlockSpec((width, cols), lambda j, b, t: (j, 0)),
                  pl.BlockSpec((1, cols), lambda j, b, t: (0, j))],
        out_specs=pl.BlockSpec((tt, cols), lambda j, b, t: (b * nt + t, j)),
        out_shape=jax.ShapeDtypeStruct((n, d), BF16),
        scratch_shapes=[pltpu.VMEM((cols, width), BF16), pltpu.VMEM((cols, width), BF16),
                        pltpu.VMEM((rr, nlb, 1, LANES), F32), pltpu.VMEM((rr, nlb, 1, LANES), F32),
                        pltpu.VMEM((nlb, rr, nchunk, LANES), F32), pltpu.VMEM((nlb, rr, nchunk, LANES), F32),
                        pltpu.VMEM((tt, width), BF16), pltpu.VMEM((tt, width), BF16),
                        pltpu.VMEM((nlb, nchunk, LANES), F32), pltpu.VMEM((nlb, nchunk, LANES), F32),
                        pltpu.VMEM((nlb, nchunk, LANES), F32), pltpu.VMEM((nlb, nchunk, LANES), F32),
                        pltpu.VMEM((nlb, 1, LANES), F32), pltpu.VMEM((nlb, 1, LANES), F32)],
        compiler_params=_params("parallel", "arbitrary", "arbitrary"),
        name="s5_scan",
    )(u, perm, perm.T, flat(lam_re), flat(lam_im), ldt, block_diag_b(b_re), block_diag_b(b_im),
      block_diag_c(c_re), block_diag_c(c_im), d_skip.astype(F32).reshape(1, d))


def _retention_kernel(q_ref, k_ref, v_ref, g_ref, cos_ref, sin_ref, gn_ref, lg_ref, o_ref,
                      state_ref, *, chunks):
    c = RET_CHUNK
    half = q_ref.shape[1] // 2

    @pl.when(pl.program_id(2) == 0)
    def _():
        state_ref[...] = jnp.zeros(state_ref.shape, F32)

    lg = lg_ref[0][:1, :1]
    ri = lax.broadcasted_iota(jnp.int32, (c, c), 0)
    ci = lax.broadcasted_iota(jnp.int32, (c, c), 1)
    rel = (ri - ci).astype(F32)
    decay = jnp.where(rel >= 0, jnp.exp(lg * jnp.maximum(rel, 0.0)), 0.0)
    idx = lax.broadcasted_iota(jnp.int32, (c, 1), 0).astype(F32)
    q_decay = jnp.exp(lg * (idx + 1.0))
    k_decay = jnp.exp(lg * (c - 1.0 - idx))
    chunk_decay = jnp.exp(lg * c)
    k_scale = q_ref.shape[1] ** -0.5

    def rope(x, cos, sin):
        x1, x2 = x[:, :half], x[:, half:]
        return jnp.concatenate([x1 * cos - x2 * sin, x2 * cos + x1 * sin], axis=1)

    for ch in range(chunks):
        rows = slice(ch * c, (ch + 1) * c)
        cos, sin = cos_ref[rows, :], sin_ref[rows, :]
        q = rope(q_ref[rows, :].astype(F32), cos, sin)
        k = rope(k_ref[rows, :].astype(F32), cos, sin) * k_scale
        v = v_ref[rows, :]
        qb = q.astype(BF16)
        inner = lax.dot_general(qb, k.astype(BF16), (((1,), (1,)), ((), ())),
                                preferred_element_type=F32) * decay
        y = jnp.dot(inner.astype(BF16), v, preferred_element_type=F32)
        state = state_ref[...]
        y = y + jnp.dot(qb, state.astype(BF16), preferred_element_type=F32) * q_decay
        kd = (k * k_decay).astype(BF16)
        state_ref[...] = state * chunk_decay + lax.dot_general(
            kd, v, (((0,), (0,)), ((), ())), preferred_element_type=F32)
        mu = jnp.mean(y, axis=1, keepdims=True)
        yc = y - mu
        var = jnp.mean(yc * yc, axis=1, keepdims=True)
        yn = yc * lax.rsqrt(var + NORM_EPS) * gn_ref[...]
        o_ref[rows, :] = (g_ref[rows, :].astype(F32) * yn).astype(o_ref.dtype)


def retention(qk, v, g, cos, sin, gn, log_gamma_tiles, batch, seq, rows_per_step=512):
    n = qk.shape[0]
    dqk = qk.shape[1] // (2 * RET_HEADS)
    dv = v.shape[1] // RET_HEADS
    rows = min(rows_per_step, seq)
    ns = seq // rows
    row_qk = lambda off: pl.BlockSpec((rows, dqk), lambda b, h, i: (b * ns + i, off + h))
    row_v = pl.BlockSpec((rows, dv), lambda b, h, i: (b * ns + i, h))
    row_t = pl.BlockSpec((rows, LANES), lambda b, h, i: (b * ns + i, 0))
    return pl.pallas_call(
        functools.partial(_retention_kernel, chunks=rows // RET_CHUNK),
        grid=(batch, RET_HEADS, ns),
        in_specs=[row_qk(0), row_qk(RET_HEADS), row_v, row_v, row_t, row_t,
                  pl.BlockSpec((1, dv), lambda b, h, i: (0, h)),
                  pl.BlockSpec((1, 8, LANES), lambda b, h, i: (h, 0, 0))],
        out_specs=row_v,
        out_shape=jax.ShapeDtypeStruct((n, RET_HEADS * dv), BF16),
        scratch_shapes=[pltpu.VMEM((dqk, dv), F32)],
        compiler_params=_params("parallel", "parallel", "arbitrary"),
        name="retention",
    )(qk, qk, v, g, cos, sin, gn.astype(F32).reshape(1, -1), log_gamma_tiles)


def _mla_mixer(tn, rope_cs, wq_a, q_norm, wq_b, wkv_a, kv_norm, wkv_b, wo, batch, seq):
    kv_rank = kv_norm.shape[0]
    qk_dim = MLA_NOPE + MLA_ROPE
    wq_b3 = wq_b.reshape(-1, MLA_HEADS, qk_dim)
    wq_b_perm = jnp.concatenate([wq_b3[:, :, :MLA_NOPE].reshape(-1, MLA_HEADS * MLA_NOPE),
                                 wq_b3[:, :, MLA_NOPE:].reshape(-1, MLA_HEADS * MLA_ROPE)], axis=1)
    wkv_b3 = wkv_b.reshape(-1, MLA_HEADS, MLA_NOPE + MLA_V)
    wkv_b_perm = jnp.concatenate([wkv_b3[:, :, :MLA_NOPE].reshape(-1, MLA_HEADS * MLA_NOPE),
                                  wkv_b3[:, :, MLA_NOPE:].reshape(-1, MLA_HEADS * MLA_V)], axis=1)
    w_kpe = wkv_a[:, kv_rank:]
    w_kpe2 = jnp.concatenate([w_kpe, w_kpe], axis=1)

    c_q = matmul(tn, wq_a.astype(BF16), out_dtype=F32)
    c_q = rmsnorm(c_q, q_norm)
    q_all = matmul(c_q, wq_b_perm.astype(BF16), out_dtype=BF16, tm=2048,
                   scale=qk_dim ** -0.5 * math.log2(math.e))
    c_kv = matmul(tn, wkv_a[:, :kv_rank].astype(BF16), out_dtype=F32)
    c_kv = rmsnorm(c_kv, kv_norm)
    kv_all = matmul(c_kv, wkv_b_perm.astype(BF16), out_dtype=BF16, tm=2048, tn=2048)
    k_pe = matmul(tn, w_kpe2.astype(BF16), out_dtype=F32)

    cos, sin = rope_cs
    k_pe = rope_pairs(k_pe, 0, 1, cos, sin)
    o = mla_attention(q_all, kv_all, k_pe, cos, sin, batch, seq)
    return matmul(o, wo.astype(BF16), out_dtype=F32)


def _s5_mixer(tn, lam_re, lam_im, log_dt, b_re, b_im, c_re, c_im, d_skip, w_glu, batch, seq):
    d = tn.shape[1]
    y = s5_scan(tn, lam_re, lam_im, log_dt, b_re, b_im, c_re, c_im, d_skip, batch, seq)
    glu_tn = 512
    return matmul(y, w_glu.astype(BF16), gate_col_block=d // glu_tn, out_dtype=F32, tn=glu_tn)


def _retention_mixer(tn, rope_cs, wq, wk, wv, wg, gn, wo, log_gamma_tiles, batch, seq):
    cos, sin = rope_cs
    qk = matmul(tn, jnp.concatenate([wq, wk], axis=1).astype(BF16), out_dtype=BF16)
    v = matmul(tn, wv.astype(BF16), out_dtype=BF16)
    g = matmul(tn, wg.astype(BF16), out_dtype=BF16, act="silu")
    y = retention(qk, v, g, cos, sin, gn, log_gamma_tiles, batch, seq)
    return matmul(y, wo.astype(BF16), out_dtype=F32)


def _rope_freqs(dim):
    return 1.0 / (ROPE_BASE ** (jnp.arange(0, dim, 2, dtype=F32) / dim))


def kernel(x, mem, positions, norm_gain, mem_norm, xa_wq, xa_wk, xa_wv, xa_wo, mlp_w1, mlp_w2,
           mla_wq_a, mla_q_norm, mla_wq_b, mla_wkv_a, mla_kv_norm, mla_wkv_b, mla_wo,
           s5_lam_re, s5_lam_im, s5_log_dt, s5_b_re, s5_b_im, s5_c_re, s5_c_im, s5_d, s5_w_glu,
           ret_wq, ret_wk, ret_wv, ret_wg, ret_gn, ret_wo):
    batch, seq, d = x.shape
    n = batch * seq
    depth = norm_gain.shape[0]
    mem_len = mem.shape[1]
    n_mixers = 3

    pos = positions.reshape(n)
    f_mla = jnp.tile(_rope_freqs(MLA_ROPE), 4)
    sign_mla = jnp.tile(jnp.concatenate([-jnp.ones(MLA_ROPE // 2, F32), jnp.ones(MLA_ROPE // 2, F32)]), 2)
    mla_cs = rope_tables(pos, f_mla, sign_mla)
    ret_cs = rope_tables(pos, _rope_freqs(d // RET_HEADS), jnp.ones(LANES, F32))
    log_gamma = jnp.log1p(-jnp.exp2(-5.0 - jnp.arange(RET_HEADS, dtype=F32)))
    log_gamma_tiles = jnp.broadcast_to(log_gamma[:, None, None], (RET_HEADS, 8, LANES))

    mem_n = rmsnorm(mem.reshape(batch * mem_len, d), mem_norm)
    w1_b = mlp_w1[0].astype(BF16)

    h = x.reshape(n, d)
    tn = rmsnorm(h, norm_gain[0, 0])
    for i in range(depth):
        kind, slot = i % n_mixers, i // n_mixers
        g = norm_gain[i]
        if kind == 0:
            t = _mla_mixer(tn, mla_cs, mla_wq_a[slot], mla_q_norm[slot], mla_wq_b[slot],
                           mla_wkv_a[slot], mla_kv_norm[slot], mla_wkv_b[slot], mla_wo[slot],
                           batch, seq)
        elif kind == 1:
            t = _s5_mixer(tn, s5_lam_re[slot], s5_lam_im[slot], s5_log_dt[slot], s5_b_re[slot],
                          s5_b_im[slot], s5_c_re[slot], s5_c_im[slot], s5_d[slot], s5_w_glu[slot],
                          batch, seq)
        else:
            t = _retention_mixer(tn, ret_cs, ret_wq[slot], ret_wk[slot], ret_wv[slot], ret_wg[slot],
                                 ret_gn[slot], ret_wo[slot], log_gamma_tiles, batch, seq)
        h = residual_norm(h, t, g[1])

        k_mem = matmul(mem_n, xa_wk[i].astype(BF16), out_dtype=BF16)
        v_mem = matmul(mem_n, xa_wv[i].astype(BF16), out_dtype=BF16)
        h, tn = cross_attention(h, xa_wq[i].astype(BF16), k_mem, v_mem, xa_wo[i].astype(BF16),
                                g[2], g[3], g[4], seq, mem_len)

        hid, w2_b = matmul(tn, w1_b, out_dtype=BF16, act="relu2", cast_stack=mlp_w2, cast_layer=i)
        if i + 1 < depth:
            t, w1_b = matmul(hid, w2_b, out_dtype=F32, cast_stack=mlp_w1, cast_layer=i + 1)
        else:
            t = matmul(hid, w2_b, out_dtype=F32)
        if i + 1 < depth:
            h, tn = residual_norm(h, t, g[5], norm_gain[i + 1, 0])
        else:
            h = residual_norm(h, t, g[5])
    return h.reshape(batch, seq, d)
```

```python
import functools
import math

import jax
import jax.numpy as jnp
from jax import lax
from jax.experimental import pallas as pl
from jax.experimental.pallas import tpu as pltpu

F32 = jnp.float32
BF16 = jnp.bfloat16

V7X_VMEM_LIMIT_BYTES = 56 * 1024 * 1024
LANES = 128

MLA_HEADS = 64
MLA_NOPE = 128
MLA_ROPE = 64
MLA_V = 128
S5_GROUP = 16
S5_STATE = 64
S5_GROUPS_PER_BLOCK = 16
RET_HEADS = 16
RET_CHUNK = 128
XA_HEADS = 4
XA_DIM = 128
ROPE_BASE = 10000.0
NORM_EPS = 1e-6


def _params(*sem):
    return pltpu.CompilerParams(dimension_semantics=sem,
                                vmem_limit_bytes=V7X_VMEM_LIMIT_BYTES)


def _rms(x, g):
    return x * lax.rsqrt(jnp.mean(x * x, axis=-1, keepdims=True) + NORM_EPS) * g


def _rmsnorm_kernel(x_ref, g_ref, o_ref):
    o_ref[...] = _rms(x_ref[...].astype(F32), g_ref[...]).astype(o_ref.dtype)


def rmsnorm(x, g, out_dtype=BF16, tm=512):
    m, d = x.shape
    tm = min(tm, m)
    return pl.pallas_call(
        _rmsnorm_kernel,
        grid=(m // tm,),
        in_specs=[pl.BlockSpec((tm, d), lambda i: (i, 0)),
                  pl.BlockSpec((1, d), lambda i: (0, 0))],
        out_specs=pl.BlockSpec((tm, d), lambda i: (i, 0)),
        out_shape=jax.ShapeDtypeStruct((m, d), out_dtype),
        compiler_params=_params("parallel"),
        name="rmsnorm",
    )(x, g.reshape(1, d).astype(F32))


def _residual_kernel(h_ref, t_ref, gpost_ref, gpre_ref, hout_ref, xn_ref):
    h = h_ref[...] + _rms(t_ref[...].astype(F32), gpost_ref[...])
    hout_ref[...] = h
    xn_ref[...] = _rms(h, gpre_ref[...]).astype(xn_ref.dtype)


def _residual_last_kernel(h_ref, t_ref, gpost_ref, hout_ref):
    hout_ref[...] = h_ref[...] + _rms(t_ref[...].astype(F32), gpost_ref[...])


def residual_norm(h, t, g_post, g_pre=None, tm=256):
    m, d = h.shape
    tm = min(tm, m)
    row = pl.BlockSpec((tm, d), lambda i: (i, 0))
    vec = pl.BlockSpec((1, d), lambda i: (0, 0))
    if g_pre is None:
        return pl.pallas_call(
            _residual_last_kernel, grid=(m // tm,),
            in_specs=[row, row, vec], out_specs=row,
            out_shape=jax.ShapeDtypeStruct((m, d), F32),
            compiler_params=_params("parallel"), name="residual_last",
        )(h, t, g_post.reshape(1, d).astype(F32))
    return pl.pallas_call(
        _residual_kernel, grid=(m // tm,),
        in_specs=[row, row, vec, vec], out_specs=[row, row],
        out_shape=[jax.ShapeDtypeStruct((m, d), F32), jax.ShapeDtypeStruct((m, d), BF16)],
        compiler_params=_params("parallel"), name="residual_norm",
    )(h, t, g_post.reshape(1, d).astype(F32), g_pre.reshape(1, d).astype(F32))


def _activation(y, act):
    if act == "relu2":
        r = jnp.maximum(y, 0.0)
        return r * r
    if act == "silu":
        return y * jax.nn.sigmoid(y)
    assert act is None
    return y


MXU_TILE_COLS = 256


def _mm_kernel(*refs, nk, act, scale, gated, side_cast):
    n_in = 2 + gated + side_cast
    x_ref, w_ref = refs[0], refs[1]
    wg_ref = refs[2] if gated else None
    o_ref = refs[n_in]
    if side_cast:
        refs[n_in + 1][...] = refs[n_in - 1][...].astype(BF16)

    if nk == 1:
        x = x_ref[...]
        y = jnp.dot(x, w_ref[...], preferred_element_type=F32)
        if scale != 1.0:
            y = y * scale
        if gated:
            y = y * jax.nn.sigmoid(jnp.dot(x, wg_ref[...], preferred_element_type=F32))
        o_ref[...] = _activation(y, act).astype(o_ref.dtype)
        return

    k = pl.program_id(2)
    chunks = [slice(c, c + MXU_TILE_COLS) for c in range(0, o_ref.shape[1], MXU_TILE_COLS)]

    @pl.when(k == 0)
    def _():
        for cols in chunks:
            o_ref[:, cols] = jnp.dot(x_ref[...], w_ref[:, cols], preferred_element_type=F32)

    @pl.when(k > 0)
    def _():
        for cols in chunks:
            o_ref[:, cols] += jnp.dot(x_ref[...], w_ref[:, cols], preferred_element_type=F32)


BF16_SUBLANES = 16


def matmul(x, w, *, out_dtype, act=None, scale=1.0, gate_col_block=None,
           cast_stack=None, cast_layer=None, tm=1024, tn=1024, tk=4096):
    m, kdim = x.shape
    gated = gate_col_block is not None
    n = w.shape[-1] // 2 if gated else w.shape[-1]
    tm, tn, tk = min(tm, m), min(tn, n), min(tk, kdim)
    assert m % tm == 0 and n % tn == 0 and kdim % tk == 0
    nk = kdim // tk
    if nk > 1:
        assert not gated and act is None and scale == 1.0 and out_dtype == F32
        assert tn % MXU_TILE_COLS == 0

    def w_spec(col0):
        return pl.BlockSpec((tk, tn), lambda i, j, k: (k, col0 + j))

    operands = [x, w] + ([w] if gated else [])
    in_specs = [pl.BlockSpec((tm, tk), lambda i, j, k: (i, k)), w_spec(0)]
    if gated:
        in_specs.append(w_spec(gate_col_block))
    out_specs = [pl.BlockSpec((tm, tn), lambda i, j, k: (i, j))]
    out_shape = [jax.ShapeDtypeStruct((m, n), out_dtype)]
    gm, gn = m // tm, n // tn
    side_cast = cast_stack is not None
    if side_cast:
        _, cast_rows, cast_cols = cast_stack.shape
        steps = gm * gn * nk
        rows = cast_rows // steps
        assert rows * steps == cast_rows and rows % BF16_SUBLANES == 0
        operands.append(cast_stack)
        in_specs.append(pl.BlockSpec((None, rows, cast_cols),
                                     lambda i, j, k: (cast_layer, (i * gn + j) * nk + k, 0)))
        out_specs.append(pl.BlockSpec((rows, cast_cols), lambda i, j, k: ((i * gn + j) * nk + k, 0)))
        out_shape.append(jax.ShapeDtypeStruct((cast_rows, cast_cols), BF16))
    res = pl.pallas_call(
        functools.partial(_mm_kernel, nk=nk, act=act, scale=scale, gated=gated, side_cast=side_cast),
        grid=(gm, gn, nk),
        in_specs=in_specs,
        out_specs=out_specs,
        out_shape=out_shape,
        compiler_params=_params("parallel", "parallel", "arbitrary"),
        name="matmul",
    )(*operands)
    return tuple(res) if side_cast else res[0]


def _rope_table_kernel(pos_ref, freq_ref, sign_ref, cos_ref, sin_ref):
    ang = pos_ref[...].astype(F32) * freq_ref[...]
    cos_ref[...] = jnp.cos(ang)
    sin_ref[...] = jnp.sin(ang) * sign_ref[...]


def rope_tables(pos, inv_freq_lanes, sign_lanes, tm=1024):
    n = pos.shape[0]
    tm = min(tm, n)
    out = jax.ShapeDtypeStruct((n, LANES), F32)
    return pl.pallas_call(
        _rope_table_kernel, grid=(n // tm,),
        in_specs=[pl.BlockSpec((tm, 1), lambda i: (i, 0)),
                  pl.BlockSpec((1, LANES), lambda i: (0, 0)),
                  pl.BlockSpec((1, LANES), lambda i: (0, 0))],
        out_specs=[pl.BlockSpec((tm, LANES), lambda i: (i, 0))] * 2,
        out_shape=[out, out],
        compiler_params=_params("parallel"), name="rope_tables",
    )(pos.reshape(n, 1), inv_freq_lanes.reshape(1, LANES), sign_lanes.reshape(1, LANES))


def _swap_half_pairs(x):
    lane = lax.broadcasted_iota(jnp.int32, x.shape, 1)
    return jnp.where(lane % MLA_ROPE < MLA_ROPE // 2,
                     pltpu.roll(x, LANES - MLA_ROPE // 2, 1),
                     pltpu.roll(x, MLA_ROPE // 2, 1))


def _rope_pairs_kernel(x_ref, cos_ref, sin_ref, o_ref):
    x = x_ref[...].astype(F32)
    o_ref[...] = (x * cos_ref[...] + _swap_half_pairs(x) * sin_ref[...]).astype(o_ref.dtype)


def rope_pairs(x, col_block0, n_col_blocks, cos, sin, tm=1024):
    n = x.shape[0]
    tm = min(tm, n)
    return pl.pallas_call(
        _rope_pairs_kernel, grid=(n // tm, n_col_blocks),
        in_specs=[pl.BlockSpec((tm, LANES), lambda i, j: (i, col_block0 + j)),
                  pl.BlockSpec((tm, LANES), lambda i, j: (i, 0)),
                  pl.BlockSpec((tm, LANES), lambda i, j: (i, 0))],
        out_specs=pl.BlockSpec((tm, LANES), lambda i, j: (i, j)),
        out_shape=jax.ShapeDtypeStruct((n, n_col_blocks * LANES), BF16),
        compiler_params=_params("parallel", "arbitrary"), name="rope_pairs",
    )(x, cos, sin)


ATTN_KEYS_PER_TILE = 256
ATTN_QUERIES_PER_TILE = 256
ATTN_SCORE_LOOKAHEAD = 8
ATTN_ONES_ROWS = 16


def _mla_attn_kernel(qn_ref, qp_ref, cos_ref, sin_ref, kn_ref, v_ref, kp_ref, o_ref,
                     kcat_ref, vt_ref, qt_ref, acc_ref, *, tq):
    ksub, qstrip = min(ATTN_KEYS_PER_TILE, tq), min(ATTN_QUERIES_PER_TILE, tq)
    qi = pl.program_id(2)
    nblk = kn_ref.shape[0] // tq
    nsub = tq // ksub
    chains = [(h, t) for h in range(2) for t in range(tq // qstrip)]

    @pl.when(qi == 0)
    def _():
        lane = lax.broadcasted_iota(jnp.int32, (tq, LANES), 1)
        for c in range(nblk):
            rows = slice(c * tq, (c + 1) * tq)
            kp = kp_ref[rows, :]
            for h in range(2):
                kcat_ref[h, rows, :MLA_NOPE] = kn_ref[rows, h * MLA_NOPE:(h + 1) * MLA_NOPE]
                keep = (lane < MLA_ROPE) if h == 0 else (lane >= MLA_ROPE)
                kcat_ref[h, rows, MLA_NOPE:] = jnp.where(keep, kp, jnp.zeros_like(kp))
                vt_ref[h, c, :MLA_V] = v_ref[rows, h * MLA_V:(h + 1) * MLA_V].T
                vt_ref[h, c, MLA_V:] = jnp.ones((ATTN_ONES_ROWS, tq), BF16)

    lane_q = lax.broadcasted_iota(jnp.int32, (tq, LANES), 1)
    qp = qp_ref[...].astype(F32)
    qp = (qp * cos_ref[...] + _swap_half_pairs(qp) * sin_ref[...]).astype(BF16)
    for h in range(2):
        keep = (lane_q < MLA_ROPE) if h == 0 else (lane_q >= MLA_ROPE)
        qcat = jnp.concatenate(
            [qn_ref[:, h * MLA_NOPE:(h + 1) * MLA_NOPE], jnp.where(keep, qp, jnp.zeros_like(qp))],
            axis=1)
        qt_ref[h] = qcat.T
    acc_ref[...] = jnp.zeros(acc_ref.shape, F32)

    krow = lax.broadcasted_iota(jnp.int32, (ksub, qstrip), 0)
    qcol = lax.broadcasted_iota(jnp.int32, (ksub, qstrip), 1)

    def scores(ci, k, mask_offset):
        h, t = chains[ci]
        s = jnp.dot(k, qt_ref[h, :, t * qstrip:(t + 1) * qstrip], preferred_element_type=F32)
        if mask_offset is not None:
            s = jnp.where(krow + mask_offset <= qcol, s, -jnp.inf)
        return s

    def update(ci, carry, s, vt):
        m_prev = carry
        m_new = jnp.maximum(m_prev, jnp.max(s, axis=0, keepdims=True))
        alpha = jnp.exp2(m_prev - m_new)
        p = jnp.exp2(s - m_new)
        acc_ref[ci] = alpha * acc_ref[ci] + jnp.dot(vt, p.astype(BF16), preferred_element_type=F32)
        return m_new

    def run_tiles(jblk, start, carries, tiles):
        carries = list(carries)
        pending = []
        for idx in range(len(tiles) + ATTN_SCORE_LOOKAHEAD):
            if idx < len(tiles):
                sub, ci, off = tiles[idx]
                k = kcat_ref[chains[ci][0], pl.ds(start + sub * ksub, ksub), :]
                pending.append(scores(ci, k, off))
            if idx >= ATTN_SCORE_LOOKAHEAD:
                done = idx - ATTN_SCORE_LOOKAHEAD
                sub, ci, off = tiles[done]
                vt = vt_ref[chains[ci][0], jblk, :, sub * ksub:(sub + 1) * ksub]
                carries[ci] = update(ci, carries[ci], pending[done], vt)
                pending[done] = None
        return tuple(carries)

    full_tiles = [(sub, ci, None) for sub in range(nsub) for ci in range(len(chains))]

    def full_block(j, carries):
        return run_tiles(j, pl.multiple_of(j * tq, tq), carries, full_tiles)

    init = tuple(jnp.full((1, qstrip), -jnp.inf, F32) for _ in chains)
    carries = lax.fori_loop(0, qi, full_block, init)

    diag_tiles = []
    for sub in range(nsub):
        for ci, (h, t) in enumerate(chains):
            if sub * ksub > (t + 1) * qstrip - 1:
                continue
            off = sub * ksub - t * qstrip
            diag_tiles.append((sub, ci, off if off + ksub - 1 > 0 else None))
    carries = run_tiles(qi, pl.multiple_of(qi * tq, tq), carries, diag_tiles)

    for ci, (h, t) in enumerate(chains):
        acc = acc_ref[ci]
        o = (acc[:MLA_V] / acc[MLA_V:MLA_V + 1]).T
        o_ref[t * qstrip:(t + 1) * qstrip, h * MLA_V:(h + 1) * MLA_V] = o.astype(o_ref.dtype)


def mla_attention(q_all, kv_all, k_pe, cos, sin, batch, seq, tq=1024):
    n = q_all.shape[0]
    tq = min(tq, seq)
    qstrip = min(ATTN_QUERIES_PER_TILE, tq)
    nq = seq // tq
    pair = 2 * MLA_NOPE
    n_pairs = MLA_HEADS // 2
    rope_block0 = MLA_HEADS * MLA_NOPE // LANES
    return pl.pallas_call(
        functools.partial(_mla_attn_kernel, tq=tq),
        grid=(batch, n_pairs, nq),
        in_specs=[
            pl.BlockSpec((tq, pair), lambda b, hp, i: (b * nq + i, hp)),
            pl.BlockSpec((tq, LANES), lambda b, hp, i: (b * nq + i, rope_block0 + hp)),
            pl.BlockSpec((tq, LANES), lambda b, hp, i: (b * nq + i, 0)),
            pl.BlockSpec((tq, LANES), lambda b, hp, i: (b * nq + i, 0)),
            pl.BlockSpec((seq, pair), lambda b, hp, i: (b, hp)),
            pl.BlockSpec((seq, pair), lambda b, hp, i: (b, n_pairs + hp)),
            pl.BlockSpec((seq, LANES), lambda b, hp, i: (b, 0)),
        ],
        out_specs=pl.BlockSpec((tq, pair), lambda b, hp, i: (b * nq + i, hp)),
        out_shape=jax.ShapeDtypeStruct((n, MLA_HEADS * MLA_V), BF16),
        scratch_shapes=[pltpu.VMEM((2, seq, 2 * LANES), BF16),
                        pltpu.VMEM((2, seq // tq, MLA_V + ATTN_ONES_ROWS, tq), BF16),
                        pltpu.VMEM((2, 2 * LANES, tq), BF16),
                        pltpu.VMEM((2 * (tq // qstrip), MLA_V + ATTN_ONES_ROWS, qstrip), F32)],
        compiler_params=_params("parallel", "parallel", "arbitrary"),
        name="mla_attention",
    )(q_all, q_all, cos, sin, kv_all, kv_all, k_pe)


def _xattn_kernel(h_ref, wq_ref, k_ref, v_ref, wo_ref, gin_ref, gpost_ref, gpre_ref, hout_ref, xn_ref):
    x = _rms(h_ref[...], gin_ref[...]).astype(BF16)
    q = jnp.dot(x, wq_ref[...], preferred_element_type=F32).astype(BF16)
    heads = []
    for h in range(XA_HEADS):
        sl = slice(h * XA_DIM, (h + 1) * XA_DIM)
        s = lax.dot_general(q[:, sl], k_ref[:, sl], (((1,), (1,)), ((), ())),
                            preferred_element_type=F32) * (XA_DIM ** -0.5)
        p = jnp.exp(s - jnp.max(s, axis=1, keepdims=True))
        p = (p / jnp.sum(p, axis=1, keepdims=True)).astype(BF16)
        heads.append(jnp.dot(p, v_ref[:, sl], preferred_element_type=F32).astype(BF16))
    o = jnp.concatenate(heads, axis=1)
    t = jnp.dot(o, wo_ref[...], preferred_element_type=F32)
    h = h_ref[...] + _rms(t, gpost_ref[...])
    hout_ref[...] = h
    xn_ref[...] = _rms(h, gpre_ref[...]).astype(xn_ref.dtype)


def cross_attention(h, wq, k, v, wo, g_in, g_post, g_pre, seq, mem_len, tm=256):
    n, d = h.shape
    xa = wq.shape[1]
    tm = min(tm, seq)
    steps_per_batch = seq // tm
    row = pl.BlockSpec((tm, d), lambda i: (i, 0))
    vec = pl.BlockSpec((1, d), lambda i: (0, 0))
    mem_blk = pl.BlockSpec((mem_len, xa), lambda i: (i // steps_per_batch, 0))
    return pl.pallas_call(
        _xattn_kernel, grid=(n // tm,),
        in_specs=[row,
                  pl.BlockSpec((d, xa), lambda i: (0, 0)),
                  mem_blk, mem_blk,
                  pl.BlockSpec((xa, d), lambda i: (0, 0)),
                  vec, vec, vec],
        out_specs=[row, row],
        out_shape=[jax.ShapeDtypeStruct((n, d), F32), jax.ShapeDtypeStruct((n, d), BF16)],
        compiler_params=_params("parallel"), name="cross_attention",
    )(h, wq, k, v, wo, *(g.reshape(1, d).astype(F32) for g in (g_in, g_post, g_pre)))


S5_ROWS_PER_CHUNK = 32


def _cmul(ar, ai, br, bi):
    return ar * br - ai * bi, ar * bi + ai * br


def _s5_kernel(u_ref, perm_ref, unperm_ref, lre_ref, lim_ref, ldt_ref, bre_ref, bim_ref,
               cre_ref, cim_ref, d_ref, y_ref,
               wbr_ref, wbi_ref, pwr_ref, pwi_ref, xr_ref, xi_ref, sr_ref, si_ref,
               zr_ref, zi_ref, pr_ref, pi_ref,
               car_ref, cai_ref, *, tt):
    t = pl.program_id(2)
    nlb, rr, nchunk = xr_ref.shape[0], xr_ref.shape[1], xr_ref.shape[2]
    lanes = [slice(lb * LANES, (lb + 1) * LANES) for lb in range(nlb)]

    @pl.when(t == 0)
    def _():
        lre, lim = lre_ref[...], lim_ref[...]
        dt = jnp.exp(ldt_ref[...])
        mag = jnp.exp(lre * dt)
        a_re, a_im = mag * jnp.cos(lim * dt), mag * jnp.sin(lim * dt)
        den = lre * lre + lim * lim
        nr, ni = a_re - 1.0, a_im
        cr = (nr * lre + ni * lim) / den
        ci = (ni * lre - nr * lim) / den
        for lb in range(nlb):
            bre, bim = bre_ref[:, lanes[lb]], bim_ref[:, lanes[lb]]
            wbr_ref[:, lanes[lb]] = (cr[lb] * bre - ci[lb] * bim).astype(BF16)
            wbi_ref[:, lanes[lb]] = (cr[lb] * bim + ci[lb] * bre).astype(BF16)
        p_re, p_im = a_re, a_im
        for r in range(rr):
            pwr_ref[r] = p_re
            pwi_ref[r] = p_im
            p_re, p_im = _cmul(p_re, p_im, a_re, a_im)
        car_ref[...] = jnp.zeros(car_ref.shape, F32)
        cai_ref[...] = jnp.zeros(cai_ref.shape, F32)

    u = jnp.dot(perm_ref[...], u_ref[...], preferred_element_type=F32).astype(BF16)
    bu_re = jnp.dot(u, wbr_ref[...], preferred_element_type=F32)
    bu_im = jnp.dot(u, wbi_ref[...], preferred_element_type=F32)
    for lb in range(nlb):
        xr_ref[lb] = bu_re[:, lanes[lb]].reshape(rr, nchunk, LANES)
        xi_ref[lb] = bu_im[:, lanes[lb]].reshape(rr, nchunk, LANES)

    for lb in range(nlb):
        ar, ai = pwr_ref[0, lb], pwi_ref[0, lb]
        pr, pi = xr_ref[lb, 0], xi_ref[lb, 0]
        for r in range(1, rr):
            mr, mi = _cmul(ar, ai, pr, pi)
            pr = mr + xr_ref[lb, r]
            pi = mi + xi_ref[lb, r]
            xr_ref[lb, r] = pr
            xi_ref[lb, r] = pi
        zr_ref[lb] = pr
        zi_ref[lb] = pi

    ac_re, ac_im = pwr_ref[rr - 1], pwi_ref[rr - 1]

    def chunk_step(c, carry):
        cr, ci = carry
        pr_ref[:, pl.ds(c, 1), :] = cr
        pi_ref[:, pl.ds(c, 1), :] = ci
        mr, mi = _cmul(ac_re, ac_im, cr, ci)
        return mr + zr_ref[:, pl.ds(c, 1), :], mi + zi_ref[:, pl.ds(c, 1), :]

    cr, ci = lax.fori_loop(0, nchunk, chunk_step, (car_ref[...], cai_ref[...]), unroll=True)
    car_ref[...] = cr
    cai_ref[...] = ci

    for lb in range(nlb):
        er, ei = pr_ref[lb], pi_ref[lb]
        for r in range(rr):
            mr, mi = _cmul(pwr_ref[r, lb], pwi_ref[r, lb], er, ei)
            rows = slice(r * nchunk, (r + 1) * nchunk)
            sr_ref[rows, lanes[lb]] = (xr_ref[lb, r] + mr).astype(BF16)
            si_ref[rows, lanes[lb]] = (xi_ref[lb, r] + mi).astype(BF16)

    y = (jnp.dot(sr_ref[...], cre_ref[...], preferred_element_type=F32)
         - jnp.dot(si_ref[...], cim_ref[...], preferred_element_type=F32))
    y = (y + d_ref[...] * u.astype(F32)).astype(BF16)
    y_ref[...] = jnp.dot(unperm_ref[...], y, preferred_element_type=F32).astype(y_ref.dtype)


def s5_scan(u, lam_re, lam_im, log_dt, b_re, b_im, c_re, c_im, d_skip, batch, seq, tt=512):
    n, d = u.shape
    groups, state = lam_re.shape
    gb = S5_GROUPS_PER_BLOCK
    nblk = groups // gb
    width = gb * state
    cols = gb * S5_GROUP
    tt = min(tt, seq)
    nt = seq // tt
    eye = jnp.eye(gb, dtype=F32)

    def block_diag_b(b):
        b4 = b.astype(F32).reshape(nblk, gb, state, S5_GROUP).transpose(0, 1, 3, 2)
        return (b4[:, :, :, None, :] * eye[None, :, None, :, None]).reshape(nblk * cols, width)

    def block_diag_c(c):
        c4 = c.astype(F32).reshape(nblk, gb, S5_GROUP, state).transpose(0, 1, 3, 2)
        return (c4[:, :, :, None, :] * eye[None, :, None, :, None]).reshape(nblk * width, cols).astype(BF16)

    nlb = width // LANES
    flat = lambda a: a.astype(F32).reshape(nblk * nlb, 1, LANES)
    ldt = flat(jnp.broadcast_to(log_dt.astype(F32)[:, None], (groups, state)))
    vec = pl.BlockSpec((nlb, 1, LANES), lambda j, b, t: (j, 0, 0))
    rr = min(S5_ROWS_PER_CHUNK, tt)
    nchunk = tt // rr
    p_idx = jnp.arange(tt)
    perm = jax.nn.one_hot((p_idx % nchunk) * rr + p_idx // nchunk, tt, dtype=BF16)
    square = pl.BlockSpec((tt, tt), lambda j, b, t: (0, 0))
    return pl.pallas_call(
        functools.partial(_s5_kernel, tt=tt),
        grid=(nblk, batch, nt),
        in_specs=[pl.BlockSpec((tt, cols), lambda j, b, t: (b * nt + t, j)),
                  square, square,
                  vec, vec, vec,
                  pl.BlockSpec((cols, width), lambda j, b, t: (j, 0)),
                  pl.BlockSpec((cols, width), lambda j, b, t: (j, 0)),
                  pl.BlockSpec((width, cols), lambda j, b, t: (j, 0)),
                  pl.BlockSpec((width, cols), lambda j, b, t: (j, 0)),
                  pl.BlockSpec((1, cols), lambda j, b, t: (0, j))],
        out_specs=pl.BlockSpec((tt, cols), lambda j, b, t: (b * nt + t, j)),
        out_shape=jax.ShapeDtypeStruct((n, d), BF16),
        scratch_shapes=[pltpu.VMEM((cols, width), BF16), pltpu.VMEM((cols, width), BF16),
                        pltpu.VMEM((rr, nlb, 1, LANES), F32), pltpu.VMEM((rr, nlb, 1, LANES), F32),
                        pltpu.VMEM((nlb, rr, nchunk, LANES), F32), pltpu.VMEM((nlb, rr, nchunk, LANES), F32),
                        pltpu.VMEM((tt, width), BF16), pltpu.VMEM((tt, width), BF16),
                        pltpu.VMEM((nlb, nchunk, LANES), F32), pltpu.VMEM((nlb, nchunk, LANES), F32),
                        pltpu.VMEM((nlb, nchunk, LANES), F32), pltpu.VMEM((nlb, nchunk, LANES), F32),
                        pltpu.VMEM((nlb, 1, LANES), F32), pltpu.VMEM((nlb, 1, LANES), F32)],
        compiler_params=_params("parallel", "arbitrary", "arbitrary"),
        name="s5_scan",
    )(u, perm, perm.T, flat(lam_re), flat(lam_im), ldt, block_diag_b(b_re), block_diag_b(b_im),
      block_diag_c(c_re), block_diag_c(c_im), d_skip.astype(F32).reshape(1, d))


def _retention_kernel(q_ref, k_ref, v_ref, g_ref, cos_ref, sin_ref, gn_ref, lg_ref, o_ref,
                      state_ref, *, chunks):
    c = RET_CHUNK
    half = q_ref.shape[1] // 2

    @pl.when(pl.program_id(2) == 0)
    def _():
        state_ref[...] = jnp.zeros(state_ref.shape, F32)

    lg = lg_ref[0][:1, :1]
    ri = lax.broadcasted_iota(jnp.int32, (c, c), 0)
    ci = lax.broadcasted_iota(jnp.int32, (c, c), 1)
    rel = (ri - ci).astype(F32)
    decay = jnp.where(rel >= 0, jnp.exp(lg * jnp.maximum(rel, 0.0)), 0.0)
    idx = lax.broadcasted_iota(jnp.int32, (c, 1), 0).astype(F32)
    q_decay = jnp.exp(lg * (idx + 1.0))
    k_decay = jnp.exp(lg * (c - 1.0 - idx))
    chunk_decay = jnp.exp(lg * c)
    k_scale = q_ref.shape[1] ** -0.5

    def rope(x, cos, sin):
        x1, x2 = x[:, :half], x[:, half:]
        return jnp.concatenate([x1 * cos - x2 * sin, x2 * cos + x1 * sin], axis=1)

    for ch in range(chunks):
        rows = slice(ch * c, (ch + 1) * c)
        cos, sin = cos_ref[rows, :], sin_ref[rows, :]
        q = rope(q_ref[rows, :].astype(F32), cos, sin)
        k = rope(k_ref[rows, :].astype(F32), cos, sin) * k_scale
        v = v_ref[rows, :]
        qb = q.astype(BF16)
        inner = lax.dot_general(qb, k.astype(BF16), (((1,), (1,)), ((), ())),
                                preferred_element_type=F32) * decay
        y = jnp.dot(inner.astype(BF16), v, preferred_element_type=F32)
        state = state_ref[...]
        y = y + jnp.dot(qb, state.astype(BF16), preferred_element_type=F32) * q_decay
        kd = (k * k_decay).astype(BF16)
        state_ref[...] = state * chunk_decay + lax.dot_general(
            kd, v, (((0,), (0,)), ((), ())), preferred_element_type=F32)
        mu = jnp.mean(y, axis=1, keepdims=True)
        yc = y - mu
        var = jnp.mean(yc * yc, axis=1, keepdims=True)
        yn = yc * lax.rsqrt(var + NORM_EPS) * gn_ref[...]
        o_ref[rows, :] = (g_ref[rows, :].astype(F32) * yn).astype(o_ref.dtype)


def retention(qk, v, g, cos, sin, gn, log_gamma_tiles, batch, seq, rows_per_step=512):
    n = qk.shape[0]
    dqk = qk.shape[1] // (2 * RET_HEADS)
    dv = v.shape[1] // RET_HEADS
    rows = min(rows_per_step, seq)
    ns = seq // rows
    row_qk = lambda off: pl.BlockSpec((rows, dqk), lambda b, h, i: (b * ns + i, off + h))
    row_v = pl.BlockSpec((rows, dv), lambda b, h, i: (b * ns + i, h))
    row_t = pl.BlockSpec((rows, LANES), lambda b, h, i: (b * ns + i, 0))
    return pl.pallas_call(
        functools.partial(_retention_kernel, chunks=rows // RET_CHUNK),
        grid=(batch, RET_HEADS, ns),
        in_specs=[row_qk(0), row_qk(RET_HEADS), row_v, row_v, row_t, row_t,
                  pl.BlockSpec((1, dv), lambda b, h, i: (0, h)),
                  pl.BlockSpec((1, 8, LANES), lambda b, h, i: (h, 0, 0))],
        out_specs=row_v,
        out_shape=jax.ShapeDtypeStruct((n, RET_HEADS * dv), BF16),
        scratch_shapes=[pltpu.VMEM((dqk, dv), F32)],
        compiler_params=_params("parallel", "parallel", "arbitrary"),
        name="retention",
    )(qk, qk, v, g, cos, sin, gn.astype(F32).reshape(1, -1), log_gamma_tiles)


def _mla_mixer(tn, rope_cs, wq_a, q_norm, wq_b, wkv_a, kv_norm, wkv_b, wo, batch, seq):
    kv_rank = kv_norm.shape[0]
    qk_dim = MLA_NOPE + MLA_ROPE
    wq_b3 = wq_b.reshape(-1, MLA_HEADS, qk_dim)
    wq_b_perm = jnp.concatenate([wq_b3[:, :, :MLA_NOPE].reshape(-1, MLA_HEADS * MLA_NOPE),
                                 wq_b3[:, :, MLA_NOPE:].reshape(-1, MLA_HEADS * MLA_ROPE)], axis=1)
    wkv_b3 = wkv_b.reshape(-1, MLA_HEADS, MLA_NOPE + MLA_V)
    wkv_b_perm = jnp.concatenate([wkv_b3[:, :, :MLA_NOPE].reshape(-1, MLA_HEADS * MLA_NOPE),
                                  wkv_b3[:, :, MLA_NOPE:].reshape(-1, MLA_HEADS * MLA_V)], axis=1)
    w_kpe = wkv_a[:, kv_rank:]
    w_kpe2 = jnp.concatenate([w_kpe, w_kpe], axis=1)

    c_q = matmul(tn, wq_a.astype(BF16), out_dtype=F32)
    c_q = rmsnorm(c_q, q_norm)
    q_all = matmul(c_q, wq_b_perm.astype(BF16), out_dtype=BF16, tm=2048,
                   scale=qk_dim ** -0.5 * math.log2(math.e))
    c_kv = matmul(tn, wkv_a[:, :kv_rank].astype(BF16), out_dtype=F32)
    c_kv = rmsnorm(c_kv, kv_norm)
    kv_all = matmul(c_kv, wkv_b_perm.astype(BF16), out_dtype=BF16, tm=2048, tn=2048)
    k_pe = matmul(tn, w_kpe2.astype(BF16), out_dtype=F32)

    cos, sin = rope_cs
    k_pe = rope_pairs(k_pe, 0, 1, cos, sin)
    o = mla_attention(q_all, kv_all, k_pe, cos, sin, batch, seq)
    return matmul(o, wo.astype(BF16), out_dtype=F32)


def _s5_mixer(tn, lam_re, lam_im, log_dt, b_re, b_im, c_re, c_im, d_skip, w_glu, batch, seq):
    d = tn.shape[1]
    y = s5_scan(tn, lam_re, lam_im, log_dt, b_re, b_im, c_re, c_im, d_skip, batch, seq)
    glu_tn = 512
    return matmul(y, w_glu.astype(BF16), gate_col_block=d // glu_tn, out_dtype=F32, tn=glu_tn)


def _retention_mixer(tn, rope_cs, wq, wk, wv, wg, gn, wo, log_gamma_tiles, batch, seq):
    cos, sin = rope_cs
    qk = matmul(tn, jnp.concatenate([wq, wk], axis=1).astype(BF16), out_dtype=BF16)
    v = matmul(tn, wv.astype(BF16), out_dtype=BF16)
    g = matmul(tn, wg.astype(BF16), out_dtype=BF16, act="silu")
    y = retention(qk, v, g, cos, sin, gn, log_gamma_tiles, batch, seq)
    return matmul(y, wo.astype(BF16), out_dtype=F32)


def _rope_freqs(dim):
    return 1.0 / (ROPE_BASE ** (jnp.arange(0, dim, 2, dtype=F32) / dim))


def kernel(x, mem, positions, norm_gain, mem_norm, xa_wq, xa_wk, xa_wv, xa_wo, mlp_w1, mlp_w2,
           mla_wq_a, mla_q_norm, mla_wq_b, mla_wkv_a, mla_kv_norm, mla_wkv_b, mla_wo,
           s5_lam_re, s5_lam_im, s5_log_dt, s5_b_re, s5_b_im, s5_c_re, s5_c_im, s5_d, s5_w_glu,
           ret_wq, ret_wk, ret_wv, ret_wg, ret_gn, ret_wo):
    batch, seq, d = x.shape
    n = batch * seq
    depth = norm_gain.shape[0]
    mem_len = mem.shape[1]
    n_mixers = 3

    pos = positions.reshape(n)
    f_mla = jnp.tile(_rope_freqs(MLA_ROPE), 4)
    sign_mla = jnp.tile(jnp.concatenate([-jnp.ones(MLA_ROPE // 2, F32), jnp.ones(MLA_ROPE // 2, F32)]), 2)
    mla_cs = rope_tables(pos, f_mla, sign_mla)
    ret_cs = rope_tables(pos, _rope_freqs(d // RET_HEADS), jnp.ones(LANES, F32))
    log_gamma = jnp.log1p(-jnp.exp2(-5.0 - jnp.arange(RET_HEADS, dtype=F32)))
    log_gamma_tiles = jnp.broadcast_to(log_gamma[:, None, None], (RET_HEADS, 8, LANES))

    mem_n = rmsnorm(mem.reshape(batch * mem_len, d), mem_norm)
    w1_b = mlp_w1[0].astype(BF16)

    h = x.reshape(n, d)
    tn = rmsnorm(h, norm_gain[0, 0])
    for i in range(depth):
        kind, slot = i % n_mixers, i // n_mixers
        g = norm_gain[i]
        if kind == 0:
            t = _mla_mixer(tn, mla_cs, mla_wq_a[slot], mla_q_norm[slot], mla_wq_b[slot],
                           mla_wkv_a[slot], mla_kv_norm[slot], mla_wkv_b[slot], mla_wo[slot],
                           batch, seq)
        elif kind == 1:
            t = _s5_mixer(tn, s5_lam_re[slot], s5_lam_im[slot], s5_log_dt[slot], s5_b_re[slot],
                          s5_b_im[slot], s5_c_re[slot], s5_c_im[slot], s5_d[slot], s5_w_glu[slot],
                          batch, seq)
        else:
            t = _retention_mixer(tn, ret_cs, ret_wq[slot], ret_wk[slot], ret_wv[slot], ret_wg[slot],
                                 ret_gn[slot], ret_wo[slot], log_gamma_tiles, batch, seq)
        h = residual_norm(h, t, g[1])

        k_mem = matmul(mem_n, xa_wk[i].astype(BF16), out_dtype=BF16)
        v_mem = matmul(mem_n, xa_wv[i].astype(BF16), out_dtype=BF16)
        h, tn = cross_attention(h, xa_wq[i].astype(BF16), k_mem, v_mem, xa_wo[i].astype(BF16),
                                g[2], g[3], g[4], seq, mem_len)

        hid, w2_b = matmul(tn, w1_b, out_dtype=BF16, act="relu2", cast_stack=mlp_w2, cast_layer=i)
        if i + 1 < depth:
            t, w1_b = matmul(hid, w2_b, out_dtype=F32, cast_stack=mlp_w1, cast_layer=i + 1)
        else:
            t = matmul(hid, w2_b, out_dtype=F32)
        if i + 1 < depth:
            h, tn = residual_norm(h, t, g[5], norm_gain[i + 1, 0])
        else:
            h = residual_norm(h, t, g[5])
    return h.reshape(batch, seq, d)
```

```python
import functools
import math

import jax
import jax.numpy as jnp
from jax import lax
from jax.experimental import pallas as pl
from jax.experimental.pallas import tpu as pltpu

F32 = jnp.float32
BF16 = jnp.bfloat16

V7X_VMEM_LIMIT_BYTES = 56 * 1024 * 1024
LANES = 128

MLA_HEADS = 64
MLA_NOPE = 128
MLA_ROPE = 64
MLA_V = 128
S5_GROUP = 16
S5_STATE = 64
S5_GROUPS_PER_BLOCK = 16
RET_HEADS = 16
RET_CHUNK = 128
XA_HEADS = 4
XA_DIM = 128
ROPE_BASE = 10000.0
NORM_EPS = 1e-6


def _params(*sem):
    return pltpu.CompilerParams(dimension_semantics=sem,
                                vmem_limit_bytes=V7X_VMEM_LIMIT_BYTES)


def _rms(x, g):
    return x * lax.rsqrt(jnp.mean(x * x, axis=-1, keepdims=True) + NORM_EPS) * g


def _rmsnorm_kernel(x_ref, g_ref, o_ref):
    o_ref[...] = _rms(x_ref[...].astype(F32), g_ref[...]).astype(o_ref.dtype)


def rmsnorm(x, g, out_dtype=BF16, tm=512):
    m, d = x.shape
    tm = min(tm, m)
    return pl.pallas_call(
        _rmsnorm_kernel,
        grid=(m // tm,),
        in_specs=[pl.BlockSpec((tm, d), lambda i: (i, 0)),
                  pl.BlockSpec((1, d), lambda i: (0, 0))],
        out_specs=pl.BlockSpec((tm, d), lambda i: (i, 0)),
        out_shape=jax.ShapeDtypeStruct((m, d), out_dtype),
        compiler_params=_params("parallel"),
        name="rmsnorm",
    )(x, g.reshape(1, d).astype(F32))


def _residual_kernel(h_ref, t_ref, gpost_ref, gpre_ref, hout_ref, xn_ref):
    h = h_ref[...] + _rms(t_ref[...].astype(F32), gpost_ref[...])
    hout_ref[...] = h
    xn_ref[...] = _rms(h, gpre_ref[...]).astype(xn_ref.dtype)


def _residual_last_kernel(h_ref, t_ref, gpost_ref, hout_ref):
    hout_ref[...] = h_ref[...] + _rms(t_ref[...].astype(F32), gpost_ref[...])


def residual_norm(h, t, g_post, g_pre=None, tm=256):
    m, d = h.shape
    tm = min(tm, m)
    row = pl.BlockSpec((tm, d), lambda i: (i, 0))
    vec = pl.BlockSpec((1, d), lambda i: (0, 0))
    if g_pre is None:
        return pl.pallas_call(
            _residual_last_kernel, grid=(m // tm,),
            in_specs=[row, row, vec], out_specs=row,
            out_shape=jax.ShapeDtypeStruct((m, d), F32),
            compiler_params=_params("parallel"), name="residual_last",
        )(h, t, g_post.reshape(1, d).astype(F32))
    return pl.pallas_call(
        _residual_kernel, grid=(m // tm,),
        in_specs=[row, row, vec, vec], out_specs=[row, row],
        out_shape=[jax.ShapeDtypeStruct((m, d), F32), jax.ShapeDtypeStruct((m, d), BF16)],
        compiler_params=_params("parallel"), name="residual_norm",
    )(h, t, g_post.reshape(1, d).astype(F32), g_pre.reshape(1, d).astype(F32))


def _activation(y, act):
    if act == "relu2":
        r = jnp.maximum(y, 0.0)
        return r * r
    if act == "silu":
        return y * jax.nn.sigmoid(y)
    assert act is None
    return y


MXU_TILE_COLS = 256


def _mm_kernel(*refs, nk, act, scale, gated, side_cast):
    n_in = 2 + gated + side_cast
    x_ref, w_ref = refs[0], refs[1]
    wg_ref = refs[2] if gated else None
    o_ref = refs[n_in]
    if side_cast:
        refs[n_in + 1][...] = refs[n_in - 1][...].astype(BF16)

    if nk == 1:
        x = x_ref[...]
        y = jnp.dot(x, w_ref[...], preferred_element_type=F32)
        if scale != 1.0:
            y = y * scale
        if gated:
            y = y * jax.nn.sigmoid(jnp.dot(x, wg_ref[...], preferred_element_type=F32))
        o_ref[...] = _activation(y, act).astype(o_ref.dtype)
        return

    k = pl.program_id(2)
    chunks = [slice(c, c + MXU_TILE_COLS) for c in range(0, o_ref.shape[1], MXU_TILE_COLS)]

    @pl.when(k == 0)
    def _():
        for cols in chunks:
            o_ref[:, cols] = jnp.dot(x_ref[...], w_ref[:, cols], preferred_element_type=F32)

    @pl.when(k > 0)
    def _():
        for cols in chunks:
            o_ref[:, cols] += jnp.dot(x_ref[...], w_ref[:, cols], preferred_element_type=F32)


BF16_SUBLANES = 16


def matmul(x, w, *, out_dtype, act=None, scale=1.0, gate_col_block=None,
           cast_stack=None, cast_layer=None, tm=1024, tn=1024, tk=4096):
    m, kdim = x.shape
    gated = gate_col_block is not None
    n = w.shape[-1] // 2 if gated else w.shape[-1]
    tm, tn, tk = min(tm, m), min(tn, n), min(tk, kdim)
    assert m % tm == 0 and n % tn == 0 and kdim % tk == 0
    nk = kdim // tk
    if nk > 1:
        assert not gated and act is None and scale == 1.0 and out_dtype == F32
        assert tn % MXU_TILE_COLS == 0

    def w_spec(col0):
        return pl.BlockSpec((tk, tn), lambda i, j, k: (k, col0 + j))

    operands = [x, w] + ([w] if gated else [])
    in_specs = [pl.BlockSpec((tm, tk), lambda i, j, k: (i, k)), w_spec(0)]
    if gated:
        in_specs.append(w_spec(gate_col_block))
    out_specs = [pl.BlockSpec((tm, tn), lambda i, j, k: (i, j))]
    out_shape = [jax.ShapeDtypeStruct((m, n), out_dtype)]
    gm, gn = m // tm, n // tn
    side_cast = cast_stack is not None
    if side_cast:
        _, cast_rows, cast_cols = cast_stack.shape
        steps = gm * gn * nk
        rows = cast_rows // steps
        assert rows * steps == cast_rows and rows % BF16_SUBLANES == 0
        operands.append(cast_stack)
        in_specs.append(pl.BlockSpec((None, rows, cast_cols),
                                     lambda i, j, k: (cast_layer, (i * gn + j) * nk + k, 0)))
        out_specs.append(pl.BlockSpec((rows, cast_cols), lambda i, j, k: ((i * gn + j) * nk + k, 0)))
        out_shape.append(jax.ShapeDtypeStruct((cast_rows, cast_cols), BF16))
    res = pl.pallas_call(
        functools.partial(_mm_kernel, nk=nk, act=act, scale=scale, gated=gated, side_cast=side_cast),
        grid=(gm, gn, nk),
        in_specs=in_specs,
        out_specs=out_specs,
        out_shape=out_shape,
        compiler_params=_params("parallel", "parallel", "arbitrary"),
        name="matmul",
    )(*operands)
    return tuple(res) if side_cast else res[0]


def _rope_table_kernel(pos_ref, freq_ref, sign_ref, cos_ref, sin_ref):
    ang = pos_ref[...].astype(F32) * freq_ref[...]
    cos_ref[...] = jnp.cos(ang)
    sin_ref[...] = jnp.sin(ang) * sign_ref[...]


def rope_tables(pos, inv_freq_lanes, sign_lanes, tm=1024):
    n = pos.shape[0]
    tm = min(tm, n)
    out = jax.ShapeDtypeStruct((n, LANES), F32)
    return pl.pallas_call(
        _rope_table_kernel, grid=(n // tm,),
        in_specs=[pl.BlockSpec((tm, 1), lambda i: (i, 0)),
                  pl.BlockSpec((1, LANES), lambda i: (0, 0)),
                  pl.BlockSpec((1, LANES), lambda i: (0, 0))],
        out_specs=[pl.BlockSpec((tm, LANES), lambda i: (i, 0))] * 2,
        out_shape=[out, out],
        compiler_params=_params("parallel"), name="rope_tables",
    )(pos.reshape(n, 1), inv_freq_lanes.reshape(1, LANES), sign_lanes.reshape(1, LANES))


def _swap_half_pairs(x):
    lane = lax.broadcasted_iota(jnp.int32, x.shape, 1)
    return jnp.where(lane % MLA_ROPE < MLA_ROPE // 2,
                     pltpu.roll(x, LANES - MLA_ROPE // 2, 1),
                     pltpu.roll(x, MLA_ROPE // 2, 1))


def _rope_pairs_kernel(x_ref, cos_ref, sin_ref, o_ref):
    x = x_ref[...].astype(F32)
    o_ref[...] = (x * cos_ref[...] + _swap_half_pairs(x) * sin_ref[...]).astype(o_ref.dtype)


def rope_pairs(x, col_block0, n_col_blocks, cos, sin, tm=1024):
    n = x.shape[0]
    tm = min(tm, n)
    return pl.pallas_call(
        _rope_pairs_kernel, grid=(n // tm, n_col_blocks),
        in_specs=[pl.BlockSpec((tm, LANES), lambda i, j: (i, col_block0 + j)),
                  pl.BlockSpec((tm, LANES), lambda i, j: (i, 0)),
                  pl.BlockSpec((tm, LANES), lambda i, j: (i, 0))],
        out_specs=pl.BlockSpec((tm, LANES), lambda i, j: (i, j)),
        out_shape=jax.ShapeDtypeStruct((n, n_col_blocks * LANES), BF16),
        compiler_params=_params("parallel", "arbitrary"), name="rope_pairs",
    )(x, cos, sin)


ATTN_KEYS_PER_TILE = 256
ATTN_QUERIES_PER_TILE = 256
ATTN_SCORE_LOOKAHEAD = 8
ATTN_ONES_ROWS = 16


def _mla_attn_kernel(qn_ref, qp_ref, cos_ref, sin_ref, kn_ref, v_ref, kp_ref, o_ref,
                     kcat_ref, vt_ref, qt_ref, acc_ref, *, tq):
    ksub, qstrip = min(ATTN_KEYS_PER_TILE, tq), min(ATTN_QUERIES_PER_TILE, tq)
    qi = pl.program_id(2)
    nblk = kn_ref.shape[0] // tq
    nsub = tq // ksub
    chains = [(h, t) for h in range(2) for t in range(tq // qstrip)]

    @pl.when(qi == 0)
    def _():
        lane = lax.broadcasted_iota(jnp.int32, (tq, LANES), 1)
        for c in range(nblk):
            rows = slice(c * tq, (c + 1) * tq)
            kp = kp_ref[rows, :]
            for h in range(2):
                kcat_ref[h, rows, :MLA_NOPE] = kn_ref[rows, h * MLA_NOPE:(h + 1) * MLA_NOPE]
                keep = (lane < MLA_ROPE) if h == 0 else (lane >= MLA_ROPE)
                kcat_ref[h, rows, MLA_NOPE:] = jnp.where(keep, kp, jnp.zeros_like(kp))
                vt_ref[h, c, :MLA_V] = v_ref[rows, h * MLA_V:(h + 1) * MLA_V].T
                vt_ref[h, c, MLA_V:] = jnp.ones((ATTN_ONES_ROWS, tq), BF16)

    lane_q = lax.broadcasted_iota(jnp.int32, (tq, LANES), 1)
    qp = qp_ref[...].astype(F32)
    qp = (qp * cos_ref[...] + _swap_half_pairs(qp) * sin_ref[...]).astype(BF16)
    for h in range(2):
        keep = (lane_q < MLA_ROPE) if h == 0 else (lane_q >= MLA_ROPE)
        qcat = jnp.concatenate(
            [qn_ref[:, h * MLA_NOPE:(h + 1) * MLA_NOPE], jnp.where(keep, qp, jnp.zeros_like(qp))],
            axis=1)
        qt_ref[h] = qcat.T
    acc_ref[...] = jnp.zeros(acc_ref.shape, F32)

    krow = lax.broadcasted_iota(jnp.int32, (ksub, qstrip), 0)
    qcol = lax.broadcasted_iota(jnp.int32, (ksub, qstrip), 1)

    def scores(ci, k, mask_offset):
        h, t = chains[ci]
        s = jnp.dot(k, qt_ref[h, :, t * qstrip:(t + 1) * qstrip], preferred_element_type=F32)
        if mask_offset is not None:
            s = jnp.where(krow + mask_offset <= qcol, s, -jnp.inf)
        return s

    def update(ci, carry, s, vt):
        m_prev = carry
        m_new = jnp.maximum(m_prev, jnp.max(s, axis=0, keepdims=True))
        alpha = jnp.exp2(m_prev - m_new)
        p = jnp.exp2(s - m_new)
        acc_ref[ci] = alpha * acc_ref[ci] + jnp.dot(vt, p.astype(BF16), preferred_element_type=F32)
        return m_new

    def run_tiles(jblk, start, carries, tiles):
        carries = list(carries)
        pending = []
        for idx in range(len(tiles) + ATTN_SCORE_LOOKAHEAD):
            if idx < len(tiles):
                sub, ci, off = tiles[idx]
                k = kcat_ref[chains[ci][0], pl.ds(start + sub * ksub, ksub), :]
                pending.append(scores(ci, k, off))
            if idx >= ATTN_SCORE_LOOKAHEAD:
                done = idx - ATTN_SCORE_LOOKAHEAD
                sub, ci, off = tiles[done]
                vt = vt_ref[chains[ci][0], jblk, :, sub * ksub:(sub + 1) * ksub]
                carries[ci] = update(ci, carries[ci], pending[done], vt)
                pending[done] = None
        return tuple(carries)

    full_tiles = [(sub, ci, None) for sub in range(nsub) for ci in range(len(chains))]

    def full_block(j, carries):
        return run_tiles(j, pl.multiple_of(j * tq, tq), carries, full_tiles)

    init = tuple(jnp.full((1, qstrip), -jnp.inf, F32) for _ in chains)
    carries = lax.fori_loop(0, qi, full_block, init)

    diag_tiles = []
    for sub in range(nsub):
        for ci, (h, t) in enumerate(chains):
            if sub * ksub > (t + 1) * qstrip - 1:
                continue
            off = sub * ksub - t * qstrip
            diag_tiles.append((sub, ci, off if off + ksub - 1 > 0 else None))
    carries = run_tiles(qi, pl.multiple_of(qi * tq, tq), carries, diag_tiles)

    for ci, (h, t) in enumerate(chains):
        acc = acc_ref[ci]
        o = (acc[:MLA_V] / acc[MLA_V:MLA_V + 1]).T
        o_ref[t * qstrip:(t + 1) * qstrip, h * MLA_V:(h + 1) * MLA_V] = o.astype(o_ref.dtype)


def mla_attention(q_all, kv_all, k_pe, cos, sin, batch, seq, tq=1024):
    n = q_all.shape[0]
    tq = min(tq, seq)
    qstrip = min(ATTN_QUERIES_PER_TILE, tq)
    nq = seq // tq
    pair = 2 * MLA_NOPE
    n_pairs = MLA_HEADS // 2
    rope_block0 = MLA_HEADS * MLA_NOPE // LANES
    return pl.pallas_call(
        functools.partial(_mla_attn_kernel, tq=tq),
        grid=(batch, n_pairs, nq),
        in_specs=[
            pl.BlockSpec((tq, pair), lambda b, hp, i: (b * nq + i, hp)),
            pl.BlockSpec((tq, LANES), lambda b, hp, i: (b * nq + i, rope_block0 + hp)),
            pl.BlockSpec((tq, LANES), lambda b, hp, i: (b * nq + i, 0)),
            pl.BlockSpec((tq, LANES), lambda b, hp, i: (b * nq + i, 0)),
            pl.BlockSpec((seq, pair), lambda b, hp, i: (b, hp)),
            pl.BlockSpec((seq, pair), lambda b, hp, i: (b, n_pairs + hp)),
            pl.BlockSpec((seq, LANES), lambda b, hp, i: (b, 0)),
        ],
        out_specs=pl.BlockSpec((tq, pair), lambda b, hp, i: (b * nq + i, hp)),
        out_shape=jax.ShapeDtypeStruct((n, MLA_HEADS * MLA_V), BF16),
        scratch_shapes=[pltpu.VMEM((2, seq, 2 * LANES), BF16),
                        pltpu.VMEM((2, seq // tq, MLA_V + ATTN_ONES_ROWS, tq), BF16),
                        pltpu.VMEM((2, 2 * LANES, tq), BF16),
                        pltpu.VMEM((2 * (tq // qstrip), MLA_V + ATTN_ONES_ROWS, qstrip), F32)],
        compiler_params=_params("parallel", "parallel", "arbitrary"),
        name="mla_attention",
    )(q_all, q_all, cos, sin, kv_all, kv_all, k_pe)


def _xattn_kernel(h_ref, tmix_ref, wq_ref, k_ref, v_ref, wo_ref, gmix_ref, gin_ref, gpost_ref, gpre_ref,
                  hout_ref, xn_ref):
    h_in = h_ref[...] + _rms(tmix_ref[...], gmix_ref[...])
    x = _rms(h_in, gin_ref[...]).astype(BF16)
    q = jnp.dot(x, wq_ref[...], preferred_element_type=F32).astype(BF16)
    heads = []
    for h in range(XA_HEADS):
        sl = slice(h * XA_DIM, (h + 1) * XA_DIM)
        s = lax.dot_general(q[:, sl], k_ref[:, sl], (((1,), (1,)), ((), ())),
                            preferred_element_type=F32) * (XA_DIM ** -0.5)
        p = jnp.exp(s - jnp.max(s, axis=1, keepdims=True))
        p = (p / jnp.sum(p, axis=1, keepdims=True)).astype(BF16)
        heads.append(jnp.dot(p, v_ref[:, sl], preferred_element_type=F32).astype(BF16))
    o = jnp.concatenate(heads, axis=1)
    t = jnp.dot(o, wo_ref[...], preferred_element_type=F32)
    h = h_in + _rms(t, gpost_ref[...])
    hout_ref[...] = h
    xn_ref[...] = _rms(h, gpre_ref[...]).astype(xn_ref.dtype)


def cross_attention(h, t_mix, wq, k, v, wo, g_mix, g_in, g_post, g_pre, seq, mem_len, tm=256):
    n, d = h.shape
    xa = wq.shape[1]
    tm = min(tm, seq)
    steps_per_batch = seq // tm
    row = pl.BlockSpec((tm, d), lambda i: (i, 0))
    vec = pl.BlockSpec((1, d), lambda i: (0, 0))
    mem_blk = pl.BlockSpec((mem_len, xa), lambda i: (i // steps_per_batch, 0))
    const = lambda shape: pl.BlockSpec(shape, lambda i: (0, 0), pipeline_mode=pl.Buffered(1))
    gains = (g_mix, g_in, g_post, g_pre)
    return pl.pallas_call(
        _xattn_kernel, grid=(n // tm,),
        in_specs=[row, row, const((d, xa)), mem_blk, mem_blk, const((xa, d))] + [vec] * len(gains),
        out_specs=[row, row],
        out_shape=[jax.ShapeDtypeStruct((n, d), F32), jax.ShapeDtypeStruct((n, d), BF16)],
        compiler_params=_params("parallel"), name="cross_attention",
    )(h, t_mix, wq, k, v, wo, *(g.reshape(1, d).astype(F32) for g in gains))


S5_ROWS_PER_CHUNK = 32


def _cmul(ar, ai, br, bi):
    return ar * br - ai * bi, ar * bi + ai * br


def _s5_kernel(u_ref, perm_ref, unperm_ref, lre_ref, lim_ref, ldt_ref, bre_ref, bim_ref,
               cre_ref, cim_ref, d_ref, y_ref,
               wbr_ref, wbi_ref, pwr_ref, pwi_ref, xr_ref, xi_ref, sr_ref, si_ref,
               zr_ref, zi_ref, pr_ref, pi_ref,
               car_ref, cai_ref, *, tt):
    t = pl.program_id(2)
    nlb, rr, nchunk = xr_ref.shape[0], xr_ref.shape[1], xr_ref.shape[2]
    lanes = [slice(lb * LANES, (lb + 1) * LANES) for lb in range(nlb)]

    @pl.when(t == 0)
    def _():
        lre, lim = lre_ref[...], lim_ref[...]
        dt = jnp.exp(ldt_ref[...])
        mag = jnp.exp(lre * dt)
        a_re, a_im = mag * jnp.cos(lim * dt), mag * jnp.sin(lim * dt)
        den = lre * lre + lim * lim
        nr, ni = a_re - 1.0, a_im
        cr = (nr * lre + ni * lim) / den
        ci = (ni * lre - nr * lim) / den
        for lb in range(nlb):
            bre, bim = bre_ref[:, lanes[lb]], bim_ref[:, lanes[lb]]
            wbr_ref[:, lanes[lb]] = (cr[lb] * bre - ci[lb] * bim).astype(BF16)
            wbi_ref[:, lanes[lb]] = (cr[lb] * bim + ci[lb] * bre).astype(BF16)
        p_re, p_im = a_re, a_im
        for r in range(rr):
            pwr_ref[r] = p_re
            pwi_ref[r] = p_im
            p_re, p_im = _cmul(p_re, p_im, a_re, a_im)
        car_ref[...] = jnp.zeros(car_ref.shape, F32)
        cai_ref[...] = jnp.zeros(cai_ref.shape, F32)

    u = jnp.dot(perm_ref[...], u_ref[...], preferred_element_type=F32).astype(BF16)
    bu_re = jnp.dot(u, wbr_ref[...], preferred_element_type=F32)
    bu_im = jnp.dot(u, wbi_ref[...], preferred_element_type=F32)
    for lb in range(nlb):
        xr_ref[lb] = bu_re[:, lanes[lb]].reshape(rr, nchunk, LANES)
        xi_ref[lb] = bu_im[:, lanes[lb]].reshape(rr, nchunk, LANES)

    for lb in range(nlb):
        ar, ai = pwr_ref[0, lb], pwi_ref[0, lb]
        pr, pi = xr_ref[lb, 0], xi_ref[lb, 0]
        for r in range(1, rr):
            mr, mi = _cmul(ar, ai, pr, pi)
            pr = mr + xr_ref[lb, r]
            pi = mi + xi_ref[lb, r]
            xr_ref[lb, r] = pr
            xi_ref[lb, r] = pi
        zr_ref[lb] = pr
        zi_ref[lb] = pi

    ac_re, ac_im = pwr_ref[rr - 1], pwi_ref[rr - 1]

    def chunk_step(c, carry):
        cr, ci = carry
        pr_ref[:, pl.ds(c, 1), :] = cr
        pi_ref[:, pl.ds(c, 1), :] = ci
        mr, mi = _cmul(ac_re, ac_im, cr, ci)
        return mr + zr_ref[:, pl.ds(c, 1), :], mi + zi_ref[:, pl.ds(c, 1), :]

    cr, ci = lax.fori_loop(0, nchunk, chunk_step, (car_ref[...], cai_ref[...]), unroll=True)
    car_ref[...] = cr
    cai_ref[...] = ci

    for lb in range(nlb):
        er, ei = pr_ref[lb], pi_ref[lb]
        for r in range(rr):
            mr, mi = _cmul(pwr_ref[r, lb], pwi_ref[r, lb], er, ei)
            rows = slice(r * nchunk, (r + 1) * nchunk)
            sr_ref[rows, lanes[lb]] = (xr_ref[lb, r] + mr).astype(BF16)
            si_ref[rows, lanes[lb]] = (xi_ref[lb, r] + mi).astype(BF16)

    y = (jnp.dot(sr_ref[...], cre_ref[...], preferred_element_type=F32)
         - jnp.dot(si_ref[...], cim_ref[...], preferred_element_type=F32))
    y = (y + d_ref[...] * u.astype(F32)).astype(BF16)
    y_ref[...] = jnp.dot(unperm_ref[...], y, preferred_element_type=F32).astype(y_ref.dtype)


def s5_scan(u, lam_re, lam_im, log_dt, b_re, b_im, c_re, c_im, d_skip, batch, seq, tt=512):
    n, d = u.shape
    groups, state = lam_re.shape
    gb = S5_GROUPS_PER_BLOCK
    nblk = groups // gb
    width = gb * state
    cols = gb * S5_GROUP
    tt = min(tt, seq)
    nt = seq // tt
    eye = jnp.eye(gb, dtype=F32)

    def block_diag_b(b):
        b4 = b.astype(F32).reshape(nblk, gb, state, S5_GROUP).transpose(0, 1, 3, 2)
        return (b4[:, :, :, None, :] * eye[None, :, None, :, None]).reshape(nblk * cols, width)

    def block_diag_c(c):
        c4 = c.astype(F32).reshape(nblk, gb, S5_GROUP, state).transpose(0, 1, 3, 2)
        return (c4[:, :, :, None, :] * eye[None, :, None, :, None]).reshape(nblk * width, cols).astype(BF16)

    nlb = width // LANES
    flat = lambda a: a.astype(F32).reshape(nblk * nlb, 1, LANES)
    ldt = flat(jnp.broadcast_to(log_dt.astype(F32)[:, None], (groups, state)))
    vec = pl.BlockSpec((nlb, 1, LANES), lambda j, b, t: (j, 0, 0))
    rr = min(S5_ROWS_PER_CHUNK, tt)
    nchunk = tt // rr
    p_idx = jnp.arange(tt)
    perm = jax.nn.one_hot((p_idx % nchunk) * rr + p_idx // nchunk, tt, dtype=BF16)
    square = pl.BlockSpec((tt, tt), lambda j, b, t: (0, 0))
    return pl.pallas_call(
        functools.partial(_s5_kernel, tt=tt),
        grid=(nblk, batch, nt),
        in_specs=[pl.BlockSpec((tt, cols), lambda j, b, t: (b * nt + t, j)),
                  square, square,
                  vec, vec, vec,
                  pl.BlockSpec((cols, width), lambda j, b, t: (j, 0)),
                  pl.BlockSpec((cols, width), lambda j, b, t: (j, 0)),
                  pl.BlockSpec((width, cols), lambda j, b, t: (j, 0)),
                  pl.BlockSpec((width, cols), lambda j, b, t: (j, 0)),
                  pl.BlockSpec((1, cols), lambda j, b, t: (0, j))],
        out_specs=pl.BlockSpec((tt, cols), lambda j, b, t: (b * nt + t, j)),
        out_shape=jax.ShapeDtypeStruct((n, d), BF16),
        scratch_shapes=[pltpu.VMEM((cols, width), BF16), pltpu.VMEM((cols, width), BF16),
                        pltpu.VMEM((rr, nlb, 1, LANES), F32), pltpu.VMEM((rr, nlb, 1, LANES), F32),
                        pltpu.VMEM((nlb, rr, nchunk, LANES), F32), pltpu.VMEM((nlb, rr, nchunk, LANES), F32),
                        pltpu.VMEM((tt, width), BF16), pltpu.VMEM((tt, width), BF16),
                        pltpu.VMEM((nlb, nchunk, LANES), F32), pltpu.VMEM((nlb, nchunk, LANES), F32),
                        pltpu.VMEM((nlb, nchunk, LANES), F32), pltpu.VMEM((nlb, nchunk, LANES), F32),
                        pltpu.VMEM((nlb, 1, LANES), F32), pltpu.VMEM((nlb, 1, LANES), F32)],
        compiler_params=_params("parallel", "arbitrary", "arbitrary"),
        name="s5_scan",
    )(u, perm, perm.T, flat(lam_re), flat(lam_im), ldt, block_diag_b(b_re), block_diag_b(b_im),
      block_diag_c(c_re), block_diag_c(c_im), d_skip.astype(F32).reshape(1, d))


def _retention_kernel(q_ref, k_ref, v_ref, g_ref, cos_ref, sin_ref, gn_ref, lg_ref, o_ref,
                      state_ref, *, chunks):
    c = RET_CHUNK
    half = q_ref.shape[1] // 2

    @pl.when(pl.program_id(2) == 0)
    def _():
        state_ref[...] = jnp.zeros(state_ref.shape, F32)

    lg = lg_ref[0][:1, :1]
    ri = lax.broadcasted_iota(jnp.int32, (c, c), 0)
    ci = lax.broadcasted_iota(jnp.int32, (c, c), 1)
    rel = (ri - ci).astype(F32)
    decay = jnp.where(rel >= 0, jnp.exp(lg * jnp.maximum(rel, 0.0)), 0.0)
    idx = lax.broadcasted_iota(jnp.int32, (c, 1), 0).astype(F32)
    q_decay = jnp.exp(lg * (idx + 1.0))
    k_decay = jnp.exp(lg * (c - 1.0 - idx))
    chunk_decay = jnp.exp(lg * c)
    k_scale = q_ref.shape[1] ** -0.5

    def rope(x, cos, sin):
        x1, x2 = x[:, :half], x[:, half:]
        return jnp.concatenate([x1 * cos - x2 * sin, x2 * cos + x1 * sin], axis=1)

    for ch in range(chunks):
        rows = slice(ch * c, (ch + 1) * c)
        cos, sin = cos_ref[rows, :], sin_ref[rows, :]
        q = rope(q_ref[rows, :].astype(F32), cos, sin)
        k = rope(k_ref[rows, :].astype(F32), cos, sin) * k_scale
        v = v_ref[rows, :]
        qb = q.astype(BF16)
        inner = lax.dot_general(qb, k.astype(BF16), (((1,), (1,)), ((), ())),
                                preferred_element_type=F32) * decay
        y = jnp.dot(inner.astype(BF16), v, preferred_element_type=F32)
        state = state_ref[...]
        y = y + jnp.dot(qb, state.astype(BF16), preferred_element_type=F32) * q_decay
        kd = (k * k_decay).astype(BF16)
        state_ref[...] = state * chunk_decay + lax.dot_general(
            kd, v, (((0,), (0,)), ((), ())), preferred_element_type=F32)
        mu = jnp.mean(y, axis=1, keepdims=True)
        yc = y - mu
        var = jnp.mean(yc * yc, axis=1, keepdims=True)
        yn = yc * lax.rsqrt(var + NORM_EPS) * gn_ref[...]
        o_ref[rows, :] = (g_ref[rows, :].astype(F32) * yn).astype(o_ref.dtype)


def retention(qk, v, g, cos, sin, gn, log_gamma_tiles, batch, seq, rows_per_step=512):
    n = qk.shape[0]
    dqk = qk.shape[1] // (2 * RET_HEADS)
    dv = v.shape[1] // RET_HEADS
    rows = min(rows_per_step, seq)
    ns = seq // rows
    row_qk = lambda off: pl.BlockSpec((rows, dqk), lambda b, h, i: (b * ns + i, off + h))
    row_v = pl.BlockSpec((rows, dv), lambda b, h, i: (b * ns + i, h))
    row_t = pl.BlockSpec((rows, LANES), lambda b, h, i: (b * ns + i, 0))
    return pl.pallas_call(
        functools.partial(_retention_kernel, chunks=rows // RET_CHUNK),
        grid=(batch, RET_HEADS, ns),
        in_specs=[row_qk(0), row_qk(RET_HEADS), row_v, row_v, row_t, row_t,
                  pl.BlockSpec((1, dv), lambda b, h, i: (0, h)),
                  pl.BlockSpec((1, 8, LANES), lambda b, h, i: (h, 0, 0))],
        out_specs=row_v,
        out_shape=jax.ShapeDtypeStruct((n, RET_HEADS * dv), BF16),
        scratch_shapes=[pltpu.VMEM((dqk, dv), F32)],
        compiler_params=_params("parallel", "parallel", "arbitrary"),
        name="retention",
    )(qk, qk, v, g, cos, sin, gn.astype(F32).reshape(1, -1), log_gamma_tiles)


def _mla_mixer(tn, rope_cs, wq_a, q_norm, wq_b, wkv_a, kv_norm, wkv_b, wo, batch, seq):
    kv_rank = kv_norm.shape[0]
    qk_dim = MLA_NOPE + MLA_ROPE
    wq_b3 = wq_b.reshape(-1, MLA_HEADS, qk_dim)
    wq_b_perm = jnp.concatenate([wq_b3[:, :, :MLA_NOPE].reshape(-1, MLA_HEADS * MLA_NOPE),
                                 wq_b3[:, :, MLA_NOPE:].reshape(-1, MLA_HEADS * MLA_ROPE)], axis=1)
    wkv_b3 = wkv_b.reshape(-1, MLA_HEADS, MLA_NOPE + MLA_V)
    wkv_b_perm = jnp.concatenate([wkv_b3[:, :, :MLA_NOPE].reshape(-1, MLA_HEADS * MLA_NOPE),
                                  wkv_b3[:, :, MLA_NOPE:].reshape(-1, MLA_HEADS * MLA_V)], axis=1)
    w_kpe = wkv_a[:, kv_rank:]
    w_kpe2 = jnp.concatenate([w_kpe, w_kpe], axis=1)

    c_q = matmul(tn, wq_a.astype(BF16), out_dtype=F32)
    c_q = rmsnorm(c_q, q_norm)
    q_all = matmul(c_q, wq_b_perm.astype(BF16), out_dtype=BF16, tm=2048,
                   scale=qk_dim ** -0.5 * math.log2(math.e))
    c_kv = matmul(tn, wkv_a[:, :kv_rank].astype(BF16), out_dtype=F32)
    c_kv = rmsnorm(c_kv, kv_norm)
    kv_all = matmul(c_kv, wkv_b_perm.astype(BF16), out_dtype=BF16, tm=2048, tn=2048)
    k_pe = matmul(tn, w_kpe2.astype(BF16), out_dtype=F32)

    cos, sin = rope_cs
    k_pe = rope_pairs(k_pe, 0, 1, cos, sin)
    o = mla_attention(q_all, kv_all, k_pe, cos, sin, batch, seq)
    return matmul(o, wo.astype(BF16), out_dtype=F32)


def _s5_mixer(tn, lam_re, lam_im, log_dt, b_re, b_im, c_re, c_im, d_skip, w_glu, batch, seq):
    d = tn.shape[1]
    y = s5_scan(tn, lam_re, lam_im, log_dt, b_re, b_im, c_re, c_im, d_skip, batch, seq)
    glu_tn = 512
    return matmul(y, w_glu.astype(BF16), gate_col_block=d // glu_tn, out_dtype=F32, tn=glu_tn)


def _retention_mixer(tn, rope_cs, wq, wk, wv, wg, gn, wo, log_gamma_tiles, batch, seq):
    cos, sin = rope_cs
    qk = matmul(tn, jnp.concatenate([wq, wk], axis=1).astype(BF16), out_dtype=BF16)
    v = matmul(tn, wv.astype(BF16), out_dtype=BF16)
    g = matmul(tn, wg.astype(BF16), out_dtype=BF16, act="silu")
    y = retention(qk, v, g, cos, sin, gn, log_gamma_tiles, batch, seq)
    return matmul(y, wo.astype(BF16), out_dtype=F32)


def _rope_freqs(dim):
    return 1.0 / (ROPE_BASE ** (jnp.arange(0, dim, 2, dtype=F32) / dim))


def kernel(x, mem, positions, norm_gain, mem_norm, xa_wq, xa_wk, xa_wv, xa_wo, mlp_w1, mlp_w2,
           mla_wq_a, mla_q_norm, mla_wq_b, mla_wkv_a, mla_kv_norm, mla_wkv_b, mla_wo,
           s5_lam_re, s5_lam_im, s5_log_dt, s5_b_re, s5_b_im, s5_c_re, s5_c_im, s5_d, s5_w_glu,
           ret_wq, ret_wk, ret_wv, ret_wg, ret_gn, ret_wo):
    batch, seq, d = x.shape
    n = batch * seq
    depth = norm_gain.shape[0]
    mem_len = mem.shape[1]
    n_mixers = 3

    pos = positions.reshape(n)
    f_mla = jnp.tile(_rope_freqs(MLA_ROPE), 4)
    sign_mla = jnp.tile(jnp.concatenate([-jnp.ones(MLA_ROPE // 2, F32), jnp.ones(MLA_ROPE // 2, F32)]), 2)
    mla_cs = rope_tables(pos, f_mla, sign_mla)
    ret_cs = rope_tables(pos, _rope_freqs(d // RET_HEADS), jnp.ones(LANES, F32))
    log_gamma = jnp.log1p(-jnp.exp2(-5.0 - jnp.arange(RET_HEADS, dtype=F32)))
    log_gamma_tiles = jnp.broadcast_to(log_gamma[:, None, None], (RET_HEADS, 8, LANES))

    mem_n = rmsnorm(mem.reshape(batch * mem_len, d), mem_norm)
    w1_b = mlp_w1[0].astype(BF16)

    h = x.reshape(n, d)
    tn = rmsnorm(h, norm_gain[0, 0])
    for i in range(depth):
        kind, slot = i % n_mixers, i // n_mixers
        g = norm_gain[i]
        if kind == 0:
            t = _mla_mixer(tn, mla_cs, mla_wq_a[slot], mla_q_norm[slot], mla_wq_b[slot],
                           mla_wkv_a[slot], mla_kv_norm[slot], mla_wkv_b[slot], mla_wo[slot],
                           batch, seq)
        elif kind == 1:
            t = _s5_mixer(tn, s5_lam_re[slot], s5_lam_im[slot], s5_log_dt[slot], s5_b_re[slot],
                          s5_b_im[slot], s5_c_re[slot], s5_c_im[slot], s5_d[slot], s5_w_glu[slot],
                          batch, seq)
        else:
            t = _retention_mixer(tn, ret_cs, ret_wq[slot], ret_wk[slot], ret_wv[slot], ret_wg[slot],
                                 ret_gn[slot], ret_wo[slot], log_gamma_tiles, batch, seq)
        k_mem = matmul(mem_n, xa_wk[i].astype(BF16), out_dtype=BF16)
        v_mem = matmul(mem_n, xa_wv[i].astype(BF16), out_dtype=BF16)
        h, tn = cross_attention(h, t, xa_wq[i].astype(BF16), k_mem, v_mem, xa_wo[i].astype(BF16),
                                g[1], g[2], g[3], g[4], seq, mem_len)

        hid, w2_b = matmul(tn, w1_b, out_dtype=BF16, act="relu2", cast_stack=mlp_w2, cast_layer=i)
        if i + 1 < depth:
            t, w1_b = matmul(hid, w2_b, out_dtype=F32, cast_stack=mlp_w1, cast_layer=i + 1)
        else:
            t = matmul(hid, w2_b, out_dtype=F32)
        if i + 1 < depth:
            h, tn = residual_norm(h, t, g[5], norm_gain[i + 1, 0])
        else:
            h = residual_norm(h, t, g[5])
    return h.reshape(batch, seq, d)
```

```python
import functools
import math

import jax
import jax.numpy as jnp
from jax import lax
from jax.experimental import pallas as pl
from jax.experimental.pallas import tpu as pltpu

F32 = jnp.float32
BF16 = jnp.bfloat16

V7X_VMEM_LIMIT_BYTES = 56 * 1024 * 1024
LANES = 128

MLA_HEADS = 64
MLA_NOPE = 128
MLA_ROPE = 64
MLA_V = 128
S5_GROUP = 16
S5_STATE = 64
S5_GROUPS_PER_BLOCK = 16
RET_HEADS = 16
RET_CHUNK = 128
XA_HEADS = 4
XA_DIM = 128
ROPE_BASE = 10000.0
NORM_EPS = 1e-6


def _params(*sem):
    return pltpu.CompilerParams(dimension_semantics=sem,
                                vmem_limit_bytes=V7X_VMEM_LIMIT_BYTES)


def _rms(x, g):
    return x * lax.rsqrt(jnp.mean(x * x, axis=-1, keepdims=True) + NORM_EPS) * g


def _rmsnorm_kernel(x_ref, g_ref, o_ref):
    o_ref[...] = _rms(x_ref[...].astype(F32), g_ref[...]).astype(o_ref.dtype)


def rmsnorm(x, g, out_dtype=BF16, tm=512):
    m, d = x.shape
    tm = min(tm, m)
    return pl.pallas_call(
        _rmsnorm_kernel,
        grid=(m // tm,),
        in_specs=[pl.BlockSpec((tm, d), lambda i: (i, 0)),
                  pl.BlockSpec((1, d), lambda i: (0, 0))],
        out_specs=pl.BlockSpec((tm, d), lambda i: (i, 0)),
        out_shape=jax.ShapeDtypeStruct((m, d), out_dtype),
        compiler_params=_params("parallel"),
        name="rmsnorm",
    )(x, g.reshape(1, d).astype(F32))


def _residual_kernel(h_ref, t_ref, gpost_ref, gpre_ref, hout_ref, xn_ref):
    h = h_ref[...] + _rms(t_ref[...].astype(F32), gpost_ref[...])
    hout_ref[...] = h
    xn_ref[...] = _rms(h, gpre_ref[...]).astype(xn_ref.dtype)


def _residual_last_kernel(h_ref, t_ref, gpost_ref, hout_ref):
    hout_ref[...] = h_ref[...] + _rms(t_ref[...].astype(F32), gpost_ref[...])


def residual_norm(h, t, g_post, g_pre=None, tm=256):
    m, d = h.shape
    tm = min(tm, m)
    row = pl.BlockSpec((tm, d), lambda i: (i, 0))
    vec = pl.BlockSpec((1, d), lambda i: (0, 0))
    if g_pre is None:
        return pl.pallas_call(
            _residual_last_kernel, grid=(m // tm,),
            in_specs=[row, row, vec], out_specs=row,
            out_shape=jax.ShapeDtypeStruct((m, d), F32),
            compiler_params=_params("parallel"), name="residual_last",
        )(h, t, g_post.reshape(1, d).astype(F32))
    return pl.pallas_call(
        _residual_kernel, grid=(m // tm,),
        in_specs=[row, row, vec, vec], out_specs=[row, row],
        out_shape=[jax.ShapeDtypeStruct((m, d), F32), jax.ShapeDtypeStruct((m, d), BF16)],
        compiler_params=_params("parallel"), name="residual_norm",
    )(h, t, g_post.reshape(1, d).astype(F32), g_pre.reshape(1, d).astype(F32))


def _activation(y, act):
    if act == "relu2":
        r = jnp.maximum(y, 0.0)
        return r * r
    if act == "silu":
        return y * jax.nn.sigmoid(y)
    assert act is None
    return y


MXU_TILE_COLS = 256


def _mm_kernel(*refs, nk, act, scale, gated, normed, side_cast):
    n_in = 2 + gated + normed + side_cast
    x_ref, w_ref = refs[0], refs[1]
    wg_ref = refs[2] if gated else None
    gain_ref = refs[2 + gated] if normed else None
    o_ref = refs[n_in]
    if side_cast:
        refs[n_in + 1][...] = refs[n_in - 1][...].astype(BF16)

    if nk == 1:
        x = x_ref[...]
        y = jnp.dot(x, w_ref[...], preferred_element_type=F32)
        if scale != 1.0:
            y = y * scale
        if gated:
            y = y * jax.nn.sigmoid(jnp.dot(x, wg_ref[...], preferred_element_type=F32))
        if normed:
            y = _rms(y, gain_ref[...])
        o_ref[...] = _activation(y, act).astype(o_ref.dtype)
        return

    k = pl.program_id(2)
    chunks = [slice(c, c + MXU_TILE_COLS) for c in range(0, o_ref.shape[1], MXU_TILE_COLS)]

    @pl.when(k == 0)
    def _():
        for cols in chunks:
            o_ref[:, cols] = jnp.dot(x_ref[...], w_ref[:, cols], preferred_element_type=F32)

    @pl.when(k > 0)
    def _():
        for cols in chunks:
            o_ref[:, cols] += jnp.dot(x_ref[...], w_ref[:, cols], preferred_element_type=F32)


BF16_SUBLANES = 16


def matmul(x, w, *, out_dtype, act=None, scale=1.0, gate_col_block=None, norm_gain=None,
           cast_stack=None, cast_layer=None, tm=1024, tn=1024, tk=4096):
    m, kdim = x.shape
    gated = gate_col_block is not None
    normed = norm_gain is not None
    n = w.shape[-1] // 2 if gated else w.shape[-1]
    tm, tn, tk = min(tm, m), min(tn, n), min(tk, kdim)
    assert m % tm == 0 and n % tn == 0 and kdim % tk == 0
    nk = kdim // tk
    if nk > 1:
        assert not gated and not normed and act is None and scale == 1.0 and out_dtype == F32
        assert tn % MXU_TILE_COLS == 0
    assert not normed or tn == n

    def w_spec(col0):
        return pl.BlockSpec((tk, tn), lambda i, j, k: (k, col0 + j))

    operands = [x, w] + ([w] if gated else [])
    in_specs = [pl.BlockSpec((tm, tk), lambda i, j, k: (i, k)), w_spec(0)]
    if gated:
        in_specs.append(w_spec(gate_col_block))
    if normed:
        operands.append(norm_gain.reshape(1, n).astype(F32))
        in_specs.append(pl.BlockSpec((1, n), lambda i, j, k: (0, 0)))
    out_specs = [pl.BlockSpec((tm, tn), lambda i, j, k: (i, j))]
    out_shape = [jax.ShapeDtypeStruct((m, n), out_dtype)]
    gm, gn = m // tm, n // tn
    side_cast = cast_stack is not None
    if side_cast:
        _, cast_rows, cast_cols = cast_stack.shape
        steps = gm * gn * nk
        rows = cast_rows // steps
        assert rows * steps == cast_rows and rows % BF16_SUBLANES == 0
        operands.append(cast_stack)
        in_specs.append(pl.BlockSpec((None, rows, cast_cols),
                                     lambda i, j, k: (cast_layer, (i * gn + j) * nk + k, 0)))
        out_specs.append(pl.BlockSpec((rows, cast_cols), lambda i, j, k: ((i * gn + j) * nk + k, 0)))
        out_shape.append(jax.ShapeDtypeStruct((cast_rows, cast_cols), BF16))
    res = pl.pallas_call(
        functools.partial(_mm_kernel, nk=nk, act=act, scale=scale, gated=gated, normed=normed,
                          side_cast=side_cast),
        grid=(gm, gn, nk),
        in_specs=in_specs,
        out_specs=out_specs,
        out_shape=out_shape,
        compiler_params=_params("parallel", "parallel", "arbitrary"),
        name="matmul",
    )(*operands)
    return tuple(res) if side_cast else res[0]


def _rope_table_kernel(pos_ref, freq_ref, sign_ref, cos_ref, sin_ref):
    ang = pos_ref[...].astype(F32) * freq_ref[...]
    cos_ref[...] = jnp.cos(ang)
    sin_ref[...] = jnp.sin(ang) * sign_ref[...]


def rope_tables(pos, inv_freq_lanes, sign_lanes, tm=1024):
    n = pos.shape[0]
    tm = min(tm, n)
    out = jax.ShapeDtypeStruct((n, LANES), F32)
    return pl.pallas_call(
        _rope_table_kernel, grid=(n // tm,),
        in_specs=[pl.BlockSpec((tm, 1), lambda i: (i, 0)),
                  pl.BlockSpec((1, LANES), lambda i: (0, 0)),
                  pl.BlockSpec((1, LANES), lambda i: (0, 0))],
        out_specs=[pl.BlockSpec((tm, LANES), lambda i: (i, 0))] * 2,
        out_shape=[out, out],
        compiler_params=_params("parallel"), name="rope_tables",
    )(pos.reshape(n, 1), inv_freq_lanes.reshape(1, LANES), sign_lanes.reshape(1, LANES))


def _swap_half_pairs(x):
    lane = lax.broadcasted_iota(jnp.int32, x.shape, 1)
    return jnp.where(lane % MLA_ROPE < MLA_ROPE // 2,
                     pltpu.roll(x, LANES - MLA_ROPE // 2, 1),
                     pltpu.roll(x, MLA_ROPE // 2, 1))


def _rope_pairs_kernel(x_ref, cos_ref, sin_ref, o_ref):
    x = x_ref[...].astype(F32)
    o_ref[...] = (x * cos_ref[...] + _swap_half_pairs(x) * sin_ref[...]).astype(o_ref.dtype)


def rope_pairs(x, col_block0, n_col_blocks, cos, sin, tm=1024):
    n = x.shape[0]
    tm = min(tm, n)
    return pl.pallas_call(
        _rope_pairs_kernel, grid=(n // tm, n_col_blocks),
        in_specs=[pl.BlockSpec((tm, LANES), lambda i, j: (i, col_block0 + j)),
                  pl.BlockSpec((tm, LANES), lambda i, j: (i, 0)),
                  pl.BlockSpec((tm, LANES), lambda i, j: (i, 0))],
        out_specs=pl.BlockSpec((tm, LANES), lambda i, j: (i, j)),
        out_shape=jax.ShapeDtypeStruct((n, n_col_blocks * LANES), BF16),
        compiler_params=_params("parallel", "arbitrary"), name="rope_pairs",
    )(x, cos, sin)


ATTN_KEYS_PER_TILE = 256
ATTN_QUERIES_PER_TILE = 256
ATTN_SCORE_LOOKAHEAD = 8
ATTN_ONES_ROWS = 16


def _mla_attn_kernel(qn_ref, qp_ref, cos_ref, sin_ref, kn_ref, v_ref, kp_ref, o_ref,
                     kcat_ref, vt_ref, qt_ref, acc_ref, *, tq):
    ksub, qstrip = min(ATTN_KEYS_PER_TILE, tq), min(ATTN_QUERIES_PER_TILE, tq)
    qi = pl.program_id(2)
    nblk = kn_ref.shape[0] // tq
    nsub = tq // ksub
    chains = [(h, t) for h in range(2) for t in range(tq // qstrip)]

    @pl.when(qi == 0)
    def _():
        lane = lax.broadcasted_iota(jnp.int32, (tq, LANES), 1)
        for c in range(nblk):
            rows = slice(c * tq, (c + 1) * tq)
            kp = kp_ref[rows, :]
            for h in range(2):
                kcat_ref[h, rows, :MLA_NOPE] = kn_ref[rows, h * MLA_NOPE:(h + 1) * MLA_NOPE]
                keep = (lane < MLA_ROPE) if h == 0 else (lane >= MLA_ROPE)
                kcat_ref[h, rows, MLA_NOPE:] = jnp.where(keep, kp, jnp.zeros_like(kp))
                vt_ref[h, c, :MLA_V] = v_ref[rows, h * MLA_V:(h + 1) * MLA_V].T
                vt_ref[h, c, MLA_V:] = jnp.ones((ATTN_ONES_ROWS, tq), BF16)

    lane_q = lax.broadcasted_iota(jnp.int32, (tq, LANES), 1)
    qp = qp_ref[...].astype(F32)
    qp = (qp * cos_ref[...] + _swap_half_pairs(qp) * sin_ref[...]).astype(BF16)
    for h in range(2):
        keep = (lane_q < MLA_ROPE) if h == 0 else (lane_q >= MLA_ROPE)
        qcat = jnp.concatenate(
            [qn_ref[:, h * MLA_NOPE:(h + 1) * MLA_NOPE], jnp.where(keep, qp, jnp.zeros_like(qp))],
            axis=1)
        qt_ref[h] = qcat.T
    acc_ref[...] = jnp.zeros(acc_ref.shape, F32)

    krow = lax.broadcasted_iota(jnp.int32, (ksub, qstrip), 0)
    qcol = lax.broadcasted_iota(jnp.int32, (ksub, qstrip), 1)

    def scores(ci, k, mask_offset):
        h, t = chains[ci]
        s = jnp.dot(k, qt_ref[h, :, t * qstrip:(t + 1) * qstrip], preferred_element_type=F32)
        if mask_offset is not None:
            s = jnp.where(krow + mask_offset <= qcol, s, -jnp.inf)
        return s

    def update(ci, carry, s, vt):
        m_prev = carry
        m_new = jnp.maximum(m_prev, jnp.max(s, axis=0, keepdims=True))
        alpha = jnp.exp2(m_prev - m_new)
        p = jnp.exp2(s - m_new)
        acc_ref[ci] = alpha * acc_ref[ci] + jnp.dot(vt, p.astype(BF16), preferred_element_type=F32)
        return m_new

    def run_tiles(jblk, start, carries, tiles):
        carries = list(carries)
        pending = []
        for idx in range(len(tiles) + ATTN_SCORE_LOOKAHEAD):
            if idx < len(tiles):
                sub, ci, off = tiles[idx]
                k = kcat_ref[chains[ci][0], pl.ds(start + sub * ksub, ksub), :]
                pending.append(scores(ci, k, off))
            if idx >= ATTN_SCORE_LOOKAHEAD:
                done = idx - ATTN_SCORE_LOOKAHEAD
                sub, ci, off = tiles[done]
                vt = vt_ref[chains[ci][0], jblk, :, sub * ksub:(sub + 1) * ksub]
                carries[ci] = update(ci, carries[ci], pending[done], vt)
                pending[done] = None
        return tuple(carries)

    full_tiles = [(sub, ci, None) for sub in range(nsub) for ci in range(len(chains))]

    def full_block(j, carries):
        return run_tiles(j, pl.multiple_of(j * tq, tq), carries, full_tiles)

    init = tuple(jnp.full((1, qstrip), -jnp.inf, F32) for _ in chains)
    carries = lax.fori_loop(0, qi, full_block, init)

    diag_tiles = []
    for sub in range(nsub):
        for ci, (h, t) in enumerate(chains):
            if sub * ksub > (t + 1) * qstrip - 1:
                continue
            off = sub * ksub - t * qstrip
            diag_tiles.append((sub, ci, off if off + ksub - 1 > 0 else None))
    carries = run_tiles(qi, pl.multiple_of(qi * tq, tq), carries, diag_tiles)

    for ci, (h, t) in enumerate(chains):
        acc = acc_ref[ci]
        o = (acc[:MLA_V] / acc[MLA_V:MLA_V + 1]).T
        o_ref[t * qstrip:(t + 1) * qstrip, h * MLA_V:(h + 1) * MLA_V] = o.astype(o_ref.dtype)


def mla_attention(q_all, kv_all, k_pe, cos, sin, batch, seq, tq=1024):
    n = q_all.shape[0]
    tq = min(tq, seq)
    qstrip = min(ATTN_QUERIES_PER_TILE, tq)
    nq = seq // tq
    pair = 2 * MLA_NOPE
    n_pairs = MLA_HEADS // 2
    rope_block0 = MLA_HEADS * MLA_NOPE // LANES
    return pl.pallas_call(
        functools.partial(_mla_attn_kernel, tq=tq),
        grid=(batch, n_pairs, nq),
        in_specs=[
            pl.BlockSpec((tq, pair), lambda b, hp, i: (b * nq + i, hp)),
            pl.BlockSpec((tq, LANES), lambda b, hp, i: (b * nq + i, rope_block0 + hp)),
            pl.BlockSpec((tq, LANES), lambda b, hp, i: (b * nq + i, 0)),
            pl.BlockSpec((tq, LANES), lambda b, hp, i: (b * nq + i, 0)),
            pl.BlockSpec((seq, pair), lambda b, hp, i: (b, hp)),
            pl.BlockSpec((seq, pair), lambda b, hp, i: (b, n_pairs + hp)),
            pl.BlockSpec((seq, LANES), lambda b, hp, i: (b, 0)),
        ],
        out_specs=pl.BlockSpec((tq, pair), lambda b, hp, i: (b * nq + i, hp)),
        out_shape=jax.ShapeDtypeStruct((n, MLA_HEADS * MLA_V), BF16),
        scratch_shapes=[pltpu.VMEM((2, seq, 2 * LANES), BF16),
                        pltpu.VMEM((2, seq // tq, MLA_V + ATTN_ONES_ROWS, tq), BF16),
                        pltpu.VMEM((2, 2 * LANES, tq), BF16),
                        pltpu.VMEM((2 * (tq // qstrip), MLA_V + ATTN_ONES_ROWS, qstrip), F32)],
        compiler_params=_params("parallel", "parallel", "arbitrary"),
        name="mla_attention",
    )(q_all, q_all, cos, sin, kv_all, kv_all, k_pe)


def _xattn_kernel(h_ref, tmix_ref, wq_ref, k_ref, v_ref, wo_ref, gmix_ref, gin_ref, gpost_ref, gpre_ref,
                  hout_ref, xn_ref):
    h_in = h_ref[...] + _rms(tmix_ref[...], gmix_ref[...])
    x = _rms(h_in, gin_ref[...]).astype(BF16)
    q = jnp.dot(x, wq_ref[...], preferred_element_type=F32).astype(BF16)
    heads = []
    for h in range(XA_HEADS):
        sl = slice(h * XA_DIM, (h + 1) * XA_DIM)
        s = lax.dot_general(q[:, sl], k_ref[:, sl], (((1,), (1,)), ((), ())),
                            preferred_element_type=F32) * (XA_DIM ** -0.5)
        p = jnp.exp(s - jnp.max(s, axis=1, keepdims=True))
        p = (p / jnp.sum(p, axis=1, keepdims=True)).astype(BF16)
        heads.append(jnp.dot(p, v_ref[:, sl], preferred_element_type=F32).astype(BF16))
    o = jnp.concatenate(heads, axis=1)
    t = jnp.dot(o, wo_ref[...], preferred_element_type=F32)
    h = h_in + _rms(t, gpost_ref[...])
    hout_ref[...] = h
    xn_ref[...] = _rms(h, gpre_ref[...]).astype(xn_ref.dtype)


def cross_attention(h, t_mix, wq, k, v, wo, g_mix, g_in, g_post, g_pre, seq, mem_len, tm=256):
    n, d = h.shape
    xa = wq.shape[1]
    tm = min(tm, seq)
    steps_per_batch = seq // tm
    row = pl.BlockSpec((tm, d), lambda i: (i, 0))
    vec = pl.BlockSpec((1, d), lambda i: (0, 0))
    mem_blk = pl.BlockSpec((mem_len, xa), lambda i: (i // steps_per_batch, 0))
    const = lambda shape: pl.BlockSpec(shape, lambda i: (0, 0), pipeline_mode=pl.Buffered(1))
    gains = (g_mix, g_in, g_post, g_pre)
    return pl.pallas_call(
        _xattn_kernel, grid=(n // tm,),
        in_specs=[row, row, const((d, xa)), mem_blk, mem_blk, const((xa, d))] + [vec] * len(gains),
        out_specs=[row, row],
        out_shape=[jax.ShapeDtypeStruct((n, d), F32), jax.ShapeDtypeStruct((n, d), BF16)],
        compiler_params=_params("parallel"), name="cross_attention",
    )(h, t_mix, wq, k, v, wo, *(g.reshape(1, d).astype(F32) for g in gains))


S5_ROWS_PER_CHUNK = 32


def _cmul(ar, ai, br, bi):
    return ar * br - ai * bi, ar * bi + ai * br


def _s5_kernel(u_ref, perm_ref, unperm_ref, lre_ref, lim_ref, ldt_ref, bre_ref, bim_ref,
               cre_ref, cim_ref, d_ref, y_ref,
               wbr_ref, wbi_ref, pwr_ref, pwi_ref, xr_ref, xi_ref, sr_ref, si_ref,
               zr_ref, zi_ref, pr_ref, pi_ref,
               car_ref, cai_ref, *, tt):
    t = pl.program_id(2)
    nlb, rr, nchunk = xr_ref.shape[0], xr_ref.shape[1], xr_ref.shape[2]
    lanes = [slice(lb * LANES, (lb + 1) * LANES) for lb in range(nlb)]

    @pl.when(t == 0)
    def _():
        lre, lim = lre_ref[...], lim_ref[...]
        dt = jnp.exp(ldt_ref[...])
        mag = jnp.exp(lre * dt)
        a_re, a_im = mag * jnp.cos(lim * dt), mag * jnp.sin(lim * dt)
        den = lre * lre + lim * lim
        nr, ni = a_re - 1.0, a_im
        cr = (nr * lre + ni * lim) / den
        ci = (ni * lre - nr * lim) / den
        for lb in range(nlb):
            bre, bim = bre_ref[:, lanes[lb]], bim_ref[:, lanes[lb]]
            wbr_ref[:, lanes[lb]] = (cr[lb] * bre - ci[lb] * bim).astype(BF16)
            wbi_ref[:, lanes[lb]] = (cr[lb] * bim + ci[lb] * bre).astype(BF16)
        p_re, p_im = a_re, a_im
        for r in range(rr):
            pwr_ref[r] = p_re
            pwi_ref[r] = p_im
            p_re, p_im = _cmul(p_re, p_im, a_re, a_im)
        car_ref[...] = jnp.zeros(car_ref.shape, F32)
        cai_ref[...] = jnp.zeros(cai_ref.shape, F32)

    u = jnp.dot(perm_ref[...], u_ref[...], preferred_element_type=F32).astype(BF16)
    bu_re = jnp.dot(u, wbr_ref[...], preferred_element_type=F32)
    bu_im = jnp.dot(u, wbi_ref[...], preferred_element_type=F32)
    for lb in range(nlb):
        xr_ref[lb] = bu_re[:, lanes[lb]].reshape(rr, nchunk, LANES)
        xi_ref[lb] = bu_im[:, lanes[lb]].reshape(rr, nchunk, LANES)

    for lb in range(nlb):
        ar, ai = pwr_ref[0, lb], pwi_ref[0, lb]
        pr, pi = xr_ref[lb, 0], xi_ref[lb, 0]
        for r in range(1, rr):
            mr, mi = _cmul(ar, ai, pr, pi)
            pr = mr + xr_ref[lb, r]
            pi = mi + xi_ref[lb, r]
            xr_ref[lb, r] = pr
            xi_ref[lb, r] = pi
        zr_ref[lb] = pr
        zi_ref[lb] = pi

    ac_re, ac_im = pwr_ref[rr - 1], pwi_ref[rr - 1]

    def chunk_step(c, carry):
        cr, ci = carry
        pr_ref[:, pl.ds(c, 1), :] = cr
        pi_ref[:, pl.ds(c, 1), :] = ci
        mr, mi = _cmul(ac_re, ac_im, cr, ci)
        return mr + zr_ref[:, pl.ds(c, 1), :], mi + zi_ref[:, pl.ds(c, 1), :]

    cr, ci = lax.fori_loop(0, nchunk, chunk_step, (car_ref[...], cai_ref[...]), unroll=True)
    car_ref[...] = cr
    cai_ref[...] = ci

    for lb in range(nlb):
        er, ei = pr_ref[lb], pi_ref[lb]
        for r in range(rr):
            mr, mi = _cmul(pwr_ref[r, lb], pwi_ref[r, lb], er, ei)
            rows = slice(r * nchunk, (r + 1) * nchunk)
            sr_ref[rows, lanes[lb]] = (xr_ref[lb, r] + mr).astype(BF16)
            si_ref[rows, lanes[lb]] = (xi_ref[lb, r] + mi).astype(BF16)

    y = (jnp.dot(sr_ref[...], cre_ref[...], preferred_element_type=F32)
         - jnp.dot(si_ref[...], cim_ref[...], preferred_element_type=F32))
    y = (y + d_ref[...] * u.astype(F32)).astype(BF16)
    y_ref[...] = jnp.dot(unperm_ref[...], y, preferred_element_type=F32).astype(y_ref.dtype)


def s5_scan(u, lam_re, lam_im, log_dt, b_re, b_im, c_re, c_im, d_skip, batch, seq, tt=512):
    n, d = u.shape
    groups, state = lam_re.shape
    gb = S5_GROUPS_PER_BLOCK
    nblk = groups // gb
    width = gb * state
    cols = gb * S5_GROUP
    tt = min(tt, seq)
    nt = seq // tt
    eye = jnp.eye(gb, dtype=F32)

    def block_diag_b(b):
        b4 = b.astype(F32).reshape(nblk, gb, state, S5_GROUP).transpose(0, 1, 3, 2)
        return (b4[:, :, :, None, :] * eye[None, :, None, :, None]).reshape(nblk * cols, width)

    def block_diag_c(c):
        c4 = c.astype(F32).reshape(nblk, gb, S5_GROUP, state).transpose(0, 1, 3, 2)
        return (c4[:, :, :, None, :] * eye[None, :, None, :, None]).reshape(nblk * width, cols).astype(BF16)

    nlb = width // LANES
    flat = lambda a: a.astype(F32).reshape(nblk * nlb, 1, LANES)
    ldt = flat(jnp.broadcast_to(log_dt.astype(F32)[:, None], (groups, state)))
    vec = pl.BlockSpec((nlb, 1, LANES), lambda j, b, t: (j, 0, 0))
    rr = min(S5_ROWS_PER_CHUNK, tt)
    nchunk = tt // rr
    p_idx = jnp.arange(tt)
    perm = jax.nn.one_hot((p_idx % nchunk) * rr + p_idx // nchunk, tt, dtype=BF16)
    square = pl.BlockSpec((tt, tt), lambda j, b, t: (0, 0))
    return pl.pallas_call(
        functools.partial(_s5_kernel, tt=tt),
        grid=(nblk, batch, nt),
        in_specs=[pl.BlockSpec((tt, cols), lambda j, b, t: (b * nt + t, j)),
                  square, square,
                  vec, vec, vec,
                  pl.BlockSpec((cols, width), lambda j, b, t: (j, 0)),
                  pl.BlockSpec((cols, width), lambda j, b, t: (j, 0)),
                  pl.BlockSpec((width, cols), lambda j, b, t: (j, 0)),
                  pl.BlockSpec((width, cols), lambda j, b, t: (j, 0)),
                  pl.BlockSpec((1, cols), lambda j, b, t: (0, j))],
        out_specs=pl.BlockSpec((tt, cols), lambda j, b, t: (b * nt + t, j)),
        out_shape=jax.ShapeDtypeStruct((n, d), BF16),
        scratch_shapes=[pltpu.VMEM((cols, width), BF16), pltpu.VMEM((cols, width), BF16),
                        pltpu.VMEM((rr, nlb, 1, LANES), F32), pltpu.VMEM((rr, nlb, 1, LANES), F32),
                        pltpu.VMEM((nlb, rr, nchunk, LANES), F32), pltpu.VMEM((nlb, rr, nchunk, LANES), F32),
                        pltpu.VMEM((tt, width), BF16), pltpu.VMEM((tt, width), BF16),
                        pltpu.VMEM((nlb, nchunk, LANES), F32), pltpu.VMEM((nlb, nchunk, LANES), F32),
                        pltpu.VMEM((nlb, nchunk, LANES), F32), pltpu.VMEM((nlb, nchunk, LANES), F32),
                        pltpu.VMEM((nlb, 1, LANES), F32), pltpu.VMEM((nlb, 1, LANES), F32)],
        compiler_params=_params("parallel", "arbitrary", "arbitrary"),
        name="s5_scan",
    )(u, perm, perm.T, flat(lam_re), flat(lam_im), ldt, block_diag_b(b_re), block_diag_b(b_im),
      block_diag_c(c_re), block_diag_c(c_im), d_skip.astype(F32).reshape(1, d))


def _retention_kernel(q_ref, k_ref, v_ref, g_ref, cos_ref, sin_ref, gn_ref, lg_ref, o_ref,
                      state_ref, *, chunks):
    c = RET_CHUNK
    half = q_ref.shape[1] // 2

    @pl.when(pl.program_id(2) == 0)
    def _():
        state_ref[...] = jnp.zeros(state_ref.shape, F32)

    lg = lg_ref[0][:1, :1]
    ri = lax.broadcasted_iota(jnp.int32, (c, c), 0)
    ci = lax.broadcasted_iota(jnp.int32, (c, c), 1)
    rel = (ri - ci).astype(F32)
    decay = jnp.where(rel >= 0, jnp.exp(lg * jnp.maximum(rel, 0.0)), 0.0)
    idx = lax.broadcasted_iota(jnp.int32, (c, 1), 0).astype(F32)
    q_decay = jnp.exp(lg * (idx + 1.0))
    k_decay = jnp.exp(lg * (c - 1.0 - idx))
    chunk_decay = jnp.exp(lg * c)
    k_scale = q_ref.shape[1] ** -0.5

    def rope(x, cos, sin):
        x1, x2 = x[:, :half], x[:, half:]
        return jnp.concatenate([x1 * cos - x2 * sin, x2 * cos + x1 * sin], axis=1)

    for ch in range(chunks):
        rows = slice(ch * c, (ch + 1) * c)
        cos, sin = cos_ref[rows, :], sin_ref[rows, :]
        q = rope(q_ref[rows, :].astype(F32), cos, sin)
        k = rope(k_ref[rows, :].astype(F32), cos, sin) * k_scale
        v = v_ref[rows, :]
        qb = q.astype(BF16)
        inner = lax.dot_general(qb, k.astype(BF16), (((1,), (1,)), ((), ())),
                                preferred_element_type=F32) * decay
        y = jnp.dot(inner.astype(BF16), v, preferred_element_type=F32)
        state = state_ref[...]
        y = y + jnp.dot(qb, state.astype(BF16), preferred_element_type=F32) * q_decay
        kd = (k * k_decay).astype(BF16)
        state_ref[...] = state * chunk_decay + lax.dot_general(
            kd, v, (((0,), (0,)), ((), ())), preferred_element_type=F32)
        mu = jnp.mean(y, axis=1, keepdims=True)
        yc = y - mu
        var = jnp.mean(yc * yc, axis=1, keepdims=True)
        yn = yc * lax.rsqrt(var + NORM_EPS) * gn_ref[...]
        o_ref[rows, :] = (g_ref[rows, :].astype(F32) * yn).astype(o_ref.dtype)


def retention(qk, v, g, cos, sin, gn, log_gamma_tiles, batch, seq, rows_per_step=512):
    n = qk.shape[0]
    dqk = qk.shape[1] // (2 * RET_HEADS)
    dv = v.shape[1] // RET_HEADS
    rows = min(rows_per_step, seq)
    ns = seq // rows
    row_qk = lambda off: pl.BlockSpec((rows, dqk), lambda b, h, i: (b * ns + i, off + h))
    row_v = pl.BlockSpec((rows, dv), lambda b, h, i: (b * ns + i, h))
    row_t = pl.BlockSpec((rows, LANES), lambda b, h, i: (b * ns + i, 0))
    return pl.pallas_call(
        functools.partial(_retention_kernel, chunks=rows // RET_CHUNK),
        grid=(batch, RET_HEADS, ns),
        in_specs=[row_qk(0), row_qk(RET_HEADS), row_v, row_v, row_t, row_t,
                  pl.BlockSpec((1, dv), lambda b, h, i: (0, h)),
                  pl.BlockSpec((1, 8, LANES), lambda b, h, i: (h, 0, 0))],
        out_specs=row_v,
        out_shape=jax.ShapeDtypeStruct((n, RET_HEADS * dv), BF16),
        scratch_shapes=[pltpu.VMEM((dqk, dv), F32)],
        compiler_params=_params("parallel", "parallel", "arbitrary"),
        name="retention",
    )(qk, qk, v, g, cos, sin, gn.astype(F32).reshape(1, -1), log_gamma_tiles)


def _mla_mixer(tn, rope_cs, wq_a, q_norm, wq_b, wkv_a, kv_norm, wkv_b, wo, batch, seq):
    kv_rank = kv_norm.shape[0]
    qk_dim = MLA_NOPE + MLA_ROPE
    wq_b3 = wq_b.reshape(-1, MLA_HEADS, qk_dim)
    wq_b_perm = jnp.concatenate([wq_b3[:, :, :MLA_NOPE].reshape(-1, MLA_HEADS * MLA_NOPE),
                                 wq_b3[:, :, MLA_NOPE:].reshape(-1, MLA_HEADS * MLA_ROPE)], axis=1)
    wkv_b3 = wkv_b.reshape(-1, MLA_HEADS, MLA_NOPE + MLA_V)
    wkv_b_perm = jnp.concatenate([wkv_b3[:, :, :MLA_NOPE].reshape(-1, MLA_HEADS * MLA_NOPE),
                                  wkv_b3[:, :, MLA_NOPE:].reshape(-1, MLA_HEADS * MLA_V)], axis=1)
    w_kpe = wkv_a[:, kv_rank:]
    w_kpe2 = jnp.concatenate([w_kpe, w_kpe], axis=1)

    c_q = matmul(tn, wq_a.astype(BF16), out_dtype=BF16, norm_gain=q_norm)
    q_all = matmul(c_q, wq_b_perm.astype(BF16), out_dtype=BF16, tm=2048,
                   scale=qk_dim ** -0.5 * math.log2(math.e))
    c_kv = matmul(tn, wkv_a[:, :kv_rank].astype(BF16), out_dtype=BF16, norm_gain=kv_norm)
    kv_all = matmul(c_kv, wkv_b_perm.astype(BF16), out_dtype=BF16, tm=2048, tn=2048)
    k_pe = matmul(tn, w_kpe2.astype(BF16), out_dtype=F32)

    cos, sin = rope_cs
    k_pe = rope_pairs(k_pe, 0, 1, cos, sin)
    o = mla_attention(q_all, kv_all, k_pe, cos, sin, batch, seq)
    return matmul(o, wo.astype(BF16), out_dtype=F32)


def _s5_mixer(tn, lam_re, lam_im, log_dt, b_re, b_im, c_re, c_im, d_skip, w_glu, batch, seq):
    d = tn.shape[1]
    y = s5_scan(tn, lam_re, lam_im, log_dt, b_re, b_im, c_re, c_im, d_skip, batch, seq)
    glu_tn = 512
    return matmul(y, w_glu.astype(BF16), gate_col_block=d // glu_tn, out_dtype=F32, tn=glu_tn)


def _retention_mixer(tn, rope_cs, wq, wk, wv, wg, gn, wo, log_gamma_tiles, batch, seq):
    cos, sin = rope_cs
    qk = matmul(tn, jnp.concatenate([wq, wk], axis=1).astype(BF16), out_dtype=BF16)
    v = matmul(tn, wv.astype(BF16), out_dtype=BF16)
    g = matmul(tn, wg.astype(BF16), out_dtype=BF16, act="silu")
    y = retention(qk, v, g, cos, sin, gn, log_gamma_tiles, batch, seq)
    return matmul(y, wo.astype(BF16), out_dtype=F32)


def _rope_freqs(dim):
    return 1.0 / (ROPE_BASE ** (jnp.arange(0, dim, 2, dtype=F32) / dim))


def kernel(x, mem, positions, norm_gain, mem_norm, xa_wq, xa_wk, xa_wv, xa_wo, mlp_w1, mlp_w2,
           mla_wq_a, mla_q_norm, mla_wq_b, mla_wkv_a, mla_kv_norm, mla_wkv_b, mla_wo,
           s5_lam_re, s5_lam_im, s5_log_dt, s5_b_re, s5_b_im, s5_c_re, s5_c_im, s5_d, s5_w_glu,
           ret_wq, ret_wk, ret_wv, ret_wg, ret_gn, ret_wo):
    batch, seq, d = x.shape
    n = batch * seq
    depth = norm_gain.shape[0]
    mem_len = mem.shape[1]
    n_mixers = 3

    pos = positions.reshape(n)
    f_mla = jnp.tile(_rope_freqs(MLA_ROPE), 4)
    sign_mla = jnp.tile(jnp.concatenate([-jnp.ones(MLA_ROPE // 2, F32), jnp.ones(MLA_ROPE // 2, F32)]), 2)
    mla_cs = rope_tables(pos, f_mla, sign_mla)
    ret_cs = rope_tables(pos, _rope_freqs(d // RET_HEADS), jnp.ones(LANES, F32))
    log_gamma = jnp.log1p(-jnp.exp2(-5.0 - jnp.arange(RET_HEADS, dtype=F32)))
    log_gamma_tiles = jnp.broadcast_to(log_gamma[:, None, None], (RET_HEADS, 8, LANES))

    mem_n = rmsnorm(mem.reshape(batch * mem_len, d), mem_norm)
    w1_b = mlp_w1[0].astype(BF16)

    h = x.reshape(n, d)
    tn = rmsnorm(h, norm_gain[0, 0])
    for i in range(depth):
        kind, slot = i % n_mixers, i // n_mixers
        g = norm_gain[i]
        if kind == 0:
            t = _mla_mixer(tn, mla_cs, mla_wq_a[slot], mla_q_norm[slot], mla_wq_b[slot],
                           mla_wkv_a[slot], mla_kv_norm[slot], mla_wkv_b[slot], mla_wo[slot],
                           batch, seq)
        elif kind == 1:
            t = _s5_mixer(tn, s5_lam_re[slot], s5_lam_im[slot], s5_log_dt[slot], s5_b_re[slot],
                          s5_b_im[slot], s5_c_re[slot], s5_c_im[slot], s5_d[slot], s5_w_glu[slot],
                          batch, seq)
        else:
            t = _retention_mixer(tn, ret_cs, ret_wq[slot], ret_wk[slot], ret_wv[slot], ret_wg[slot],
                                 ret_gn[slot], ret_wo[slot], log_gamma_tiles, batch, seq)
        k_mem = matmul(mem_n, xa_wk[i].astype(BF16), out_dtype=BF16)
        v_mem = matmul(mem_n, xa_wv[i].astype(BF16), out_dtype=BF16)
        h, tn = cross_attention(h, t, xa_wq[i].astype(BF16), k_mem, v_mem, xa_wo[i].astype(BF16),
                                g[1], g[2], g[3], g[4], seq, mem_len)

        hid, w2_b = matmul(tn, w1_b, out_dtype=BF16, act="relu2", cast_stack=mlp_w2, cast_layer=i)
        if i + 1 < depth:
            t, w1_b = matmul(hid, w2_b, out_dtype=F32, cast_stack=mlp_w1, cast_layer=i + 1)
        else:
            t = matmul(hid, w2_b, out_dtype=F32)
        if i + 1 < depth:
            h, tn = residual_norm(h, t, g[5], norm_gain[i + 1, 0])
        else:
            h = residual_norm(h, t, g[5])
    return h.reshape(batch, seq, d)
```
